```python
import math
import jax, jax.numpy as jnp
from jax import lax
import numpy as np

D_MODEL = 1024
BATCH = 4
SEQ = 4096
DEPTH = 1
DEC_BATCH = 128
DEC_SEQ = 8
PAST_LEN = 8192
PAGE_SIZE = 128

A_GROUPS = 8
A_GROUP_DIM = 64
A_WIDTH = A_GROUPS * A_GROUP_DIM
CHUNK = 128
B_HEADS = 8
HEAD_DIM = 64
B_WIDTH = B_HEADS * HEAD_DIM
MOBA_BLOCK = 256
MOBA_TOPK = 3
Q_BLOCK = 64
SCALE = HEAD_DIM ** -0.5
MOE_GROUPS = 4
MOE_PER_GROUP = 4
N_EXPERTS = MOE_GROUPS * MOE_PER_GROUP
MOE_TOPK = 2
D_EXPERT = 512
EPS = 1e-6
OFF_U = 0
OFF_VA = OFF_U + A_WIDTH
OFF_Q = OFF_VA + A_WIDTH
OFF_K = OFF_Q + B_WIDTH
OFF_VB = OFF_K + B_WIDTH
OFF_GATE = OFF_VB + B_WIDTH
IN_COLS = OFF_GATE + 2 * D_MODEL

kernel_name = 'hybrid_sgu_moba_hmoe_step'


def rms_norm(x, g):
    xf = x.astype(jnp.float32)
    y = xf * lax.rsqrt(jnp.mean(xf * xf, axis=-1, keepdims=True) + EPS)
    return (y * g.astype(jnp.float32)).astype(x.dtype)


def layer_norm(x, g, b):
    xf = x.astype(jnp.float32)
    xc = xf - jnp.mean(xf, axis=-1, keepdims=True)
    y = xc * lax.rsqrt(jnp.mean(xc * xc, axis=-1, keepdims=True) + EPS)
    return (y * g.astype(jnp.float32) + b.astype(jnp.float32)).astype(x.dtype)


def softmax_f32(s, dtype):
    return jax.nn.softmax(s.astype(jnp.float32), axis=-1).astype(dtype)


def mixer_inputs(x, norm1_g, w_in, b_gate, ln_v_g, ln_v_b, qn_g, kn_g):
    lead = x.shape[:-1]
    z = rms_norm(x, norm1_g) @ w_in
    u = jax.nn.gelu(z[..., OFF_U:OFF_VA])
    va = jax.nn.gelu(z[..., OFF_VA:OFF_Q]).reshape(*lead, A_GROUPS, A_GROUP_DIM)
    va = layer_norm(va, ln_v_g, ln_v_b)
    q = rms_norm(z[..., OFF_Q:OFF_K].reshape(*lead, B_HEADS, HEAD_DIM), qn_g)
    k = rms_norm(z[..., OFF_K:OFF_VB].reshape(*lead, B_HEADS, HEAD_DIM), kn_g)
    v = z[..., OFF_VB:OFF_GATE].reshape(*lead, B_HEADS, HEAD_DIM)
    gates = jax.nn.sigmoid(z[..., OFF_GATE:] + b_gate)
    return u, va, q, k, v, gates


def spatial_gate(u, va, w_sp, b_sp):
    L = va.shape[-3]
    causal = jnp.tril(jnp.ones((L, L), dtype=bool))
    w = jnp.where(causal[None], w_sp[:, :L, :L], 0.0).astype(va.dtype)
    mixed = jnp.einsum('gts,...sgd->...tgd', w, va) + b_sp[:, :L].T[:, :, None].astype(va.dtype)
    return u * mixed.reshape(*mixed.shape[:-2], A_WIDTH)


def gated_merge(y_a, y_b, gates, w_pa, w_pb, w_o):
    return (gates[..., :D_MODEL] * (y_a @ w_pa) + gates[..., D_MODEL:] * (y_b @ w_pb)) @ w_o


def moba_prompt(q, k, v):
    bsz, seq = q.shape[:2]
    nb = -(-seq // MOBA_BLOCK)
    pad = nb * MOBA_BLOCK - seq

    def blocks(t):
        t = jnp.pad(t, ((0, 0), (0, pad), (0, 0), (0, 0)))
        return t.reshape(bsz, nb, MOBA_BLOCK, B_HEADS, HEAD_DIM).transpose(0, 3, 1, 2, 4)

    kb, vb = blocks(k), blocks(v)
    qh = q.transpose(0, 2, 1, 3)
    q_blk = jnp.arange(seq) // MOBA_BLOCK
    n_sel = min(MOBA_TOPK, nb - 1)
    if n_sel > 0:
        kmean = jnp.mean(kb.astype(jnp.float32), axis=3)
        gate = jnp.einsum('bhsd,bhnd->bhsn', qh.astype(jnp.float32), kmean)
        gate = jnp.where(jnp.arange(nb)[None, :] < q_blk[:, None], gate, -jnp.inf)
        _, sel = lax.top_k(gate, n_sel)
        sel_ok = jnp.arange(n_sel)[None, :] < q_blk[:, None]
    bi = jnp.arange(bsz)[:, None, None, None]
    hi = jnp.arange(B_HEADS)[None, :, None, None]

    def attend_rows(i):
        s0 = i * Q_BLOCK
        qc = lax.dynamic_slice_in_dim(qh, s0, Q_BLOCK, axis=2)
        blk = s0 // MOBA_BLOCK
        k_own = lax.dynamic_index_in_dim(kb, blk, axis=2, keepdims=False)
        v_own = lax.dynamic_index_in_dim(vb, blk, axis=2, keepdims=False)
        qpos = s0 + jnp.arange(Q_BLOCK)
        kpos = blk * MOBA_BLOCK + jnp.arange(MOBA_BLOCK)
        s_own = jnp.einsum('bhqd,bhkd->bhqk', qc, k_own) * SCALE
        s_own = jnp.where(kpos[None, :] <= qpos[:, None], s_own, -jnp.inf)
        if n_sel == 0:
            return jnp.einsum('bhqk,bhkd->bhqd', softmax_f32(s_own, v.dtype), v_own)
        sel_c = lax.dynamic_slice_in_dim(sel, s0, Q_BLOCK, axis=2)
        ok_c = lax.dynamic_slice_in_dim(sel_ok, s0, Q_BLOCK, axis=0)
        k_sel = kb[bi, hi, sel_c]
        v_sel = vb[bi, hi, sel_c]
        s_sel = jnp.einsum('bhqd,bhqnkd->bhqnk', qc, k_sel) * SCALE
        s_sel = jnp.where(ok_c[:, :, None], s_sel, -jnp.inf)
        n_keys = n_sel * MOBA_BLOCK
        p = softmax_f32(jnp.concatenate([s_sel.reshape(bsz, B_HEADS, Q_BLOCK, n_keys), s_own], axis=-1), v.dtype)
        p_sel = p[..., :n_keys].reshape(bsz, B_HEADS, Q_BLOCK, n_sel, MOBA_BLOCK)
        return (jnp.einsum('bhqnk,bhqnkd->bhqd', p_sel, v_sel)
                + jnp.einsum('bhqk,bhkd->bhqd', p[..., n_keys:], v_own))

    out = lax.map(attend_rows, jnp.arange(seq // Q_BLOCK))
    return out.transpose(1, 0, 3, 2, 4).reshape(bsz, seq, B_WIDTH)


def moba_sample(q, k_new, v_new, cache_k, cache_v, page_table):
    n_new = q.shape[1]
    nb_past = PAST_LEN // MOBA_BLOCK
    own_start = nb_past * MOBA_BLOCK
    n_own_cached = PAST_LEN - own_start
    pages_per_block = MOBA_BLOCK // PAGE_SIZE
    n_sel = min(MOBA_TOPK, nb_past)
    qpos = PAST_LEN + jnp.arange(n_new)
    kpos = own_start + jnp.arange(n_own_cached + n_new)
    own_mask = kpos[None, :] <= qpos[:, None]
    hi = jnp.arange(B_HEADS)[:, None, None]

    def attend_seq(args):
        qs, kn, vn, ptab = args
        own_pages = ptab[own_start // PAGE_SIZE:PAST_LEN // PAGE_SIZE]
        k_own = jnp.concatenate([cache_k[own_pages].reshape(n_own_cached, B_HEADS, HEAD_DIM), kn], axis=0)
        v_own = jnp.concatenate([cache_v[own_pages].reshape(n_own_cached, B_HEADS, HEAD_DIM), vn], axis=0)
        s_own = jnp.einsum('qhd,khd->hqk', qs, k_own) * SCALE
        s_own = jnp.where(own_mask, s_own, -jnp.inf)
        if n_sel == 0:
            return jnp.einsum('hqk,khd->qhd', softmax_f32(s_own, vn.dtype), v_own)
        kblk = cache_k[ptab[:nb_past * pages_per_block]].reshape(nb_past, MOBA_BLOCK, B_HEADS, HEAD_DIM)
        kmean = jnp.mean(kblk.astype(jnp.float32), axis=1)
        gate = jnp.einsum('qhd,nhd->hqn', qs.astype(jnp.float32), kmean)
        _, sel = lax.top_k(gate, n_sel)
        k_sel = kblk[sel, :, hi]
        tok = sel[..., None] * MOBA_BLOCK + jnp.arange(MOBA_BLOCK)
        v_sel = cache_v[ptab[tok // PAGE_SIZE], tok % PAGE_SIZE, hi[..., None]]
        s_sel = jnp.einsum('qhd,hqnkd->hqnk', qs, k_sel) * SCALE
        n_keys = n_sel * MOBA_BLOCK
        p = softmax_f32(jnp.concatenate([s_sel.reshape(B_HEADS, n_new, n_keys), s_own], axis=-1), vn.dtype)
        p_sel = p[..., :n_keys].reshape(B_HEADS, n_new, n_sel, MOBA_BLOCK)
        return (jnp.einsum('hqnk,hqnkd->qhd', p_sel, v_sel)
                + jnp.einsum('hqk,khd->qhd', p[..., n_keys:], v_own))

    out = lax.map(attend_seq, (q, k_new, v_new, page_table))
    return out.reshape(q.shape[0], n_new, B_WIDTH)


def hier_moe(x, w_rg, b_rg, w_re, b_re, w1, w3, w2):
    lead = x.shape[:-1]
    xf = x.reshape(-1, D_MODEL)
    g_logits = (xf @ w_rg + b_rg).astype(jnp.float32)
    grp = jnp.argmax(g_logits, axis=-1)
    p_grp = jnp.take_along_axis(jax.nn.softmax(g_logits, axis=-1), grp[:, None], axis=1)
    e_logits = (xf @ w_re + b_re).astype(jnp.float32).reshape(-1, MOE_GROUPS, MOE_PER_GROUP)
    e_logits = jnp.take_along_axis(e_logits, grp[:, None, None], axis=1)[:, 0]
    top_v, top_i = lax.top_k(e_logits, MOE_TOPK)
    weight = p_grp * jax.nn.softmax(top_v, axis=-1)
    expert = grp[:, None] * MOE_PER_GROUP + top_i
    combine = jnp.sum(jax.nn.one_hot(expert, N_EXPERTS, dtype=jnp.float32) * weight[..., None], axis=1).astype(x.dtype)
    h = jax.nn.silu(jnp.einsum('nd,edf->nef', xf, w1)) * jnp.einsum('nd,edf->nef', xf, w3)
    y = jnp.einsum('nef,efd->nd', h * combine[..., None], w2)
    return y.reshape(*lead, D_MODEL)


def decoder_layer(x_p, x_s, cache_k, cache_v, page_table, norm1_g, w_in, b_gate, ln_v_g, ln_v_b,
                  w_sp, b_sp, qn_g, kn_g, w_pa, w_pb, w_o, norm2_g, w_rg, b_rg, w_re, b_re, w1, w3, w2):
    mix_args = (norm1_g, w_in, b_gate, ln_v_g, ln_v_b, qn_g, kn_g)
    moe_args = (w_rg, b_rg, w_re, b_re, w1, w3, w2)
    bsz, seq = x_p.shape[:2]
    n_chunks = seq // CHUNK
    u, va, q, k_p, v_p, gates = mixer_inputs(x_p, *mix_args)
    y_a = spatial_gate(u.reshape(bsz, n_chunks, CHUNK, A_WIDTH),
                       va.reshape(bsz, n_chunks, CHUNK, A_GROUPS, A_GROUP_DIM), w_sp, b_sp).reshape(bsz, seq, A_WIDTH)
    y_b = moba_prompt(q, k_p, v_p)
    h_p = x_p + gated_merge(y_a, y_b, gates, w_pa, w_pb, w_o)
    h_p = h_p + hier_moe(rms_norm(h_p, norm2_g), *moe_args)
    u, va_s, q, k_s, v_s, gates = mixer_inputs(x_s, *mix_args)
    y_a = spatial_gate(u, va_s, w_sp, b_sp)
    y_b = moba_sample(q, k_s, v_s, cache_k, cache_v, page_table)
    h_s = x_s + gated_merge(y_a, y_b, gates, w_pa, w_pb, w_o)
    h_s = h_s + hier_moe(rms_norm(h_s, norm2_g), *moe_args)
    return h_p, h_s, k_p, v_p, k_s, v_s, va_s.reshape(x_s.shape[0], x_s.shape[1], A_WIDTH)


def setup_inputs(seed: int = 0) -> dict:
    key = jax.random.key(seed)
    ks = jax.random.split(key, 32)
    f32 = jnp.float32
    n_pages = PAST_LEN // PAGE_SIZE
    n_used = DEC_BATCH * n_pages
    n_phys = (5 * n_used + 3) // 4

    def nrm(k, shape, scale):
        return jax.random.normal(k, shape, f32) * scale

    page_table = jax.random.permutation(ks[4], n_phys)[:n_used].reshape(DEC_BATCH, n_pages).astype(jnp.int32)
    return {
        'x_prompt': nrm(ks[0], (BATCH, SEQ, D_MODEL), 1.0),
        'x_sample': nrm(ks[1], (DEC_BATCH, DEC_SEQ, D_MODEL), 1.0),
        'cache_k': nrm(ks[2], (n_phys, PAGE_SIZE, B_HEADS, HEAD_DIM), 1.0),
        'cache_v': nrm(ks[3], (n_phys, PAGE_SIZE, B_HEADS, HEAD_DIM), 1.0),
        'page_table': page_table,
        'norm1_g': 1.0 + nrm(ks[5], (D_MODEL,), 0.02),
        'w_in': nrm(ks[6], (D_MODEL, IN_COLS), D_MODEL ** -0.5),
        'b_gate': nrm(ks[7], (2 * D_MODEL,), 0.1),
        'ln_v_g': 1.0 + nrm(ks[8], (A_GROUPS, A_GROUP_DIM), 0.02),
        'ln_v_b': nrm(ks[9], (A_GROUPS, A_GROUP_DIM), 0.02),
        'w_sp': nrm(ks[10], (A_GROUPS, CHUNK, CHUNK), CHUNK ** -0.5),
        'b_sp': 1.0 + nrm(ks[11], (A_GROUPS, CHUNK), 0.1),
        'qn_g': 1.0 + nrm(ks[12], (HEAD_DIM,), 0.02),
        'kn_g': 1.0 + nrm(ks[13], (HEAD_DIM,), 0.02),
        'w_pa': nrm(ks[14], (A_WIDTH, D_MODEL), A_WIDTH ** -0.5),
        'w_pb': nrm(ks[15], (B_WIDTH, D_MODEL), B_WIDTH ** -0.5),
        'w_o': nrm(ks[16], (D_MODEL, D_MODEL), D_MODEL ** -0.5),
        'norm2_g': 1.0 + nrm(ks[17], (D_MODEL,), 0.02),
        'w_rg': nrm(ks[18], (D_MODEL, MOE_GROUPS), D_MODEL ** -0.5),
        'b_rg': nrm(ks[19], (MOE_GROUPS,), 0.01),
        'w_re': nrm(ks[20], (D_MODEL, N_EXPERTS), D_MODEL ** -0.5),
        'b_re': nrm(ks[21], (N_EXPERTS,), 0.01),
        'w1': nrm(ks[22], (N_EXPERTS, D_MODEL, D_EXPERT), D_MODEL ** -0.5),
        'w3': nrm(ks[23], (N_EXPERTS, D_MODEL, D_EXPERT), D_MODEL ** -0.5),
        'w2': nrm(ks[24], (N_EXPERTS, D_EXPERT, D_MODEL), D_EXPERT ** -0.5),
    }


def reference(x_prompt, x_sample, cache_k, cache_v, page_table, norm1_g, w_in, b_gate, ln_v_g, ln_v_b,
              w_sp, b_sp, qn_g, kn_g, w_pa, w_pb, w_o, norm2_g, w_rg, b_rg, w_re, b_re, w1, w3, w2):
    y_p, y_s = x_prompt, x_sample
    for _ in range(DEPTH):
        y_p, y_s, k_p, v_p, k_s, v_s, sgu_v_s = decoder_layer(
            y_p, y_s, cache_k, cache_v, page_table, norm1_g, w_in, b_gate, ln_v_g, ln_v_b,
            w_sp, b_sp, qn_g, kn_g, w_pa, w_pb, w_o, norm2_g, w_rg, b_rg, w_re, b_re, w1, w3, w2)
    return (y_p, y_s, k_p, v_p, k_s, v_s, sgu_v_s)
```

```python
import functools

import jax
import jax.numpy as jnp
from jax import lax
from jax.experimental import pallas as pl
from jax.experimental.pallas import tpu as pltpu

F32 = jnp.float32
BF16 = jnp.bfloat16

D_MODEL = 1024
A_GROUPS = 8
A_GROUP_DIM = 64
A_WIDTH = A_GROUPS * A_GROUP_DIM
CHUNK = 128
B_HEADS = 8
HEAD_DIM = 64
B_WIDTH = B_HEADS * HEAD_DIM
MOBA_BLOCK = 256
MOBA_TOPK = 3
PAGE_SIZE = 128
SCALE = HEAD_DIM ** -0.5
MOE_GROUPS = 4
MOE_PER_GROUP = 4
N_EXPERTS = MOE_GROUPS * MOE_PER_GROUP
D_EXPERT = 512
EPS = 1e-6
OFF_U = 0
OFF_VA = OFF_U + A_WIDTH
OFF_Q = OFF_VA + A_WIDTH
OFF_K = OFF_Q + B_WIDTH
OFF_VB = OFF_K + B_WIDTH
OFF_GATE = OFF_VB + B_WIDTH

LANES = 128
ROW_TILE = 256
MOE_ROW_TILE = 1024
PAIR = 2 * HEAD_DIM
VMEM_LIMIT = 56 * 1024 * 1024
NEG = -1e30


def _dot(a, b):
    return jnp.dot(a, b, preferred_element_type=F32)


def _dot_t(a, b):
    return lax.dot_general(a, b, (((1,), (1,)), ((), ())), preferred_element_type=F32)


def _split(a):
    hi = a.astype(BF16)
    lo = (a - hi.astype(F32)).astype(BF16)
    return hi, lo


def _dot3(a, b):
    ah, al = _split(a)
    bh, bl = _split(b)
    return _dot(ah, bh) + _dot(ah, bl) + _dot(al, bh)


def _mixer_kernel(prompt, x_ref, g1_ref, win_ref, bgate_ref, lng_ref, lnb_ref, seg_ref, wst_ref, bsp_ref,
                  qg_ref, kg_ref, wpa_ref, *outs):
    if prompt:
        pa_ref, gb_ref, k_ref, v_ref, qt_ref, vt_ref, km_ref = outs
    else:
        pa_ref, gb_ref, q_ref, k_ref, v_ref, va_ref = outs
    x = x_ref[...]
    ms = jnp.mean(x * x, axis=-1, keepdims=True)
    xb = ((x * lax.rsqrt(ms + EPS)) * g1_ref[...]).astype(BF16)
    seg = seg_ref[...]

    def proj(off, width):
        return _dot(xb, win_ref[:, off:off + width])

    def segmean(a):
        return _dot(a.astype(BF16), seg)

    u = jax.nn.gelu(proj(OFF_U, A_WIDTH))
    va = jax.nn.gelu(proj(OFF_VA, A_WIDTH))
    xc = va - segmean(va)
    va = xc * lax.rsqrt(segmean(xc * xc) + EPS) * lng_ref[...] + lnb_ref[...]
    if not prompt:
        va_ref[...] = va

    wst = wst_ref[...]
    grp = lax.broadcasted_iota(jnp.int32, (CHUNK, A_WIDTH), 1) // A_GROUP_DIM
    ya = []
    for c in range(ROW_TILE // CHUNK):
        rows = slice(c * CHUNK, (c + 1) * CHUNK)
        r = _dot(wst, va[rows].astype(BF16))
        mixed = bsp_ref[...]
        for g in range(A_GROUPS):
            mixed = mixed + jnp.where(grp == g, r[g * CHUNK:(g + 1) * CHUNK], 0.0)
        ya.append(u[rows] * mixed)
    ya = jnp.concatenate(ya, axis=0)

    gates_a = jax.nn.sigmoid(proj(OFF_GATE, D_MODEL) + bgate_ref[:, :D_MODEL])
    pa_ref[...] = gates_a * _dot(ya.astype(BF16), wpa_ref[...])
    gb_ref[...] = jax.nn.sigmoid(proj(OFF_GATE + D_MODEL, D_MODEL) + bgate_ref[:, D_MODEL:])

    zq = proj(OFF_Q, B_WIDTH)
    q = zq * lax.rsqrt(segmean(zq * zq) + EPS) * qg_ref[...]
    zk = proj(OFF_K, B_WIDTH)
    k = zk * lax.rsqrt(segmean(zk * zk) + EPS) * kg_ref[...]
    v = proj(OFF_VB, B_WIDTH)
    k_ref[...] = k
    v_ref[...] = v
    if prompt:
        qt_ref[...] = q.T
        vt_ref[...] = v.T
        km_ref[0] = jnp.mean(k, axis=0, keepdims=True)
    else:
        q_ref[...] = q


def _mixer(x2d, prm, wst, bsp, prompt):
    n = x2d.shape[0]
    assert n % ROW_TILE == 0
    nt = n // ROW_TILE
    full = lambda a: pl.BlockSpec(a.shape, lambda i: (0,) * a.ndim)
    rows = lambda w: pl.BlockSpec((ROW_TILE, w), lambda i: (i, 0))
    consts = [prm['g1'], prm['win'], prm['bgate'], prm['lng'], prm['lnb'], prm['seg'], wst, bsp,
              prm['qg'], prm['kg'], prm['wpa']]
    out_shape = [jax.ShapeDtypeStruct((n, D_MODEL), F32), jax.ShapeDtypeStruct((n, D_MODEL), F32)]
    out_specs = [rows(D_MODEL), rows(D_MODEL)]
    if prompt:
        out_shape += [jax.ShapeDtypeStruct((n, B_WIDTH), F32)] * 2
        out_specs += [rows(B_WIDTH)] * 2
        out_shape += [jax.ShapeDtypeStruct((B_WIDTH, n), F32)] * 2
        out_specs += [pl.BlockSpec((B_WIDTH, ROW_TILE), lambda i: (0, i))] * 2
        out_shape += [jax.ShapeDtypeStruct((nt, 1, B_WIDTH), F32)]
        out_specs += [pl.BlockSpec((1, 1, B_WIDTH), lambda i: (i, 0, 0))]
    else:
        out_shape += [jax.ShapeDtypeStruct((n, B_WIDTH), F32)] * 3
        out_specs += [rows(B_WIDTH)] * 3
        out_shape += [jax.ShapeDtypeStruct((n, A_WIDTH), F32)]
        out_specs += [rows(A_WIDTH)]
    return pl.pallas_call(
        functools.partial(_mixer_kernel, prompt),
        grid=(nt,),
        in_specs=[rows(D_MODEL)] + [full(a) for a in consts],
        out_specs=out_specs,
        out_shape=out_shape,
        compiler_params=pltpu.CompilerParams(dimension_semantics=("arbitrary",), vmem_limit_bytes=VMEM_LIMIT),
        name="mixer_p" if prompt else "mixer_s",
    )(x2d, *consts)


def _moba_prompt_kernel(n_sel, qt_ref, k_ref, vt_ref, km_ref, o_ref, sel_ref):
    i = pl.program_id(2)
    blk = MOBA_BLOCK
    nb = km_ref.shape[0]
    qt2 = qt_ref[...]
    km2 = km_ref[...]
    row = lax.broadcasted_iota(jnp.int32, (PAIR, blk), 0)
    brow = lax.broadcasted_iota(jnp.int32, (nb, blk), 0)
    kpos = lax.broadcasted_iota(jnp.int32, (blk, blk), 0)
    qpos = lax.broadcasted_iota(jnp.int32, (blk, blk), 1)
    halves = []
    for hh in range(2):
        qm = jnp.where((row // HEAD_DIM) == hh, qt2, 0.0)
        gate = jnp.where(brow < i, _dot3(km2, qm), -jnp.inf)
        sel = jnp.zeros((nb, blk), F32)
        for r in range(n_sel):
            mx = jnp.max(gate, axis=0, keepdims=True)
            idx = jnp.min(jnp.where(gate == mx, brow, nb), axis=0, keepdims=True)
            hit = brow == idx
            sel = jnp.where(hit & (r < i), 1.0, sel)
            gate = jnp.where(hit, -jnp.inf, gate)
        sel_ref[hh] = sel
        qs = (qm * SCALE).astype(BF16)
        hrows = slice(hh * HEAD_DIM, (hh + 1) * HEAD_DIM)

        own = pl.multiple_of(i * blk, blk)
        s = _dot(k_ref[pl.ds(own, blk), :].astype(BF16), qs)
        s = jnp.where(kpos <= qpos, s, NEG)
        m = jnp.max(s, axis=0, keepdims=True)
        p = jnp.exp(s - m)
        l = jnp.sum(p, axis=0, keepdims=True)
        acc = _dot(vt_ref[hrows, pl.ds(own, blk)].astype(BF16), p.astype(BF16))

        def body(j, carry):
            m, l, acc = carry
            off = pl.multiple_of(j * blk, blk)
            s = _dot(k_ref[pl.ds(off, blk), :].astype(BF16), qs)
            s = jnp.where(sel_ref[hh, pl.ds(j, 1), :] > 0.0, s, NEG)
            m_new = jnp.maximum(m, jnp.max(s, axis=0, keepdims=True))
            alpha = jnp.exp(m - m_new)
            p = jnp.exp(s - m_new)
            l = alpha * l + jnp.sum(p, axis=0, keepdims=True)
            acc = alpha * acc + _dot(vt_ref[hrows, pl.ds(off, blk)].astype(BF16), p.astype(BF16))
            return m_new, l, acc

        m, l, acc = lax.fori_loop(0, i, body, (m, l, acc))
        halves.append(acc / l)
    o_ref[...] = jnp.concatenate(halves, axis=0).T


def _moba_prompt(qt, k, vt, kmean, bsz, seq):
    nb = seq // MOBA_BLOCK
    n_sel = min(MOBA_TOPK, nb - 1)
    npair = B_WIDTH // PAIR
    return pl.pallas_call(
        functools.partial(_moba_prompt_kernel, n_sel),
        grid=(bsz, npair, nb),
        in_specs=[
            pl.BlockSpec((PAIR, MOBA_BLOCK), lambda b, p, i: (p, b * nb + i)),
            pl.BlockSpec((seq, PAIR), lambda b, p, i: (b, p)),
            pl.BlockSpec((PAIR, seq), lambda b, p, i: (p, b)),
            pl.BlockSpec((nb, PAIR), lambda b, p, i: (b, p)),
        ],
        out_specs=pl.BlockSpec((MOBA_BLOCK, PAIR), lambda b, p, i: (b * nb + i, p)),
        out_shape=jax.ShapeDtypeStruct((bsz * seq, B_WIDTH), F32),
        scratch_shapes=[pltpu.VMEM((2, nb, MOBA_BLOCK), F32)],
        compiler_params=pltpu.CompilerParams(dimension_semantics=("arbitrary",) * 3, vmem_limit_bytes=VMEM_LIMIT),
        name="moba_p",
    )(qt, k, vt, kmean)


SAMPLE_BLOCKS_PER_STEP = 4
PAGES_PER_BLOCK = MOBA_BLOCK // PAGE_SIZE


def _diag_fold(o_all):
    rows = o_all.shape[0]
    n_new = rows // B_HEADS
    rh = lax.broadcasted_iota(jnp.int32, (rows, B_WIDTH), 0) // n_new
    lh = lax.broadcasted_iota(jnp.int32, (rows, B_WIDTH), 1) // HEAD_DIM
    o = jnp.where(rh == lh, o_all, 0.0)
    out = o[:, 0:LANES]
    for t in range(1, B_WIDTH // LANES):
        out = out + o[:, t * LANES:(t + 1) * LANES]
    return out


def _moba_sample_kernel(n_new, nb_past, n_sel, pt_ref, q_ref, kn_ref, vn_ref, *rest):
    npg = SAMPLE_BLOCKS_PER_STEP * PAGES_PER_BLOCK
    k_refs = rest[:npg]
    v_refs = rest[npg:2 * npg]
    o_ref = rest[2 * npg]
    g_scr, m_scr, l_scr, o_scr = rest[2 * npg + 1:]
    step = pl.program_id(1)
    rows = B_HEADS * n_new
    rh = lax.broadcasted_iota(jnp.int32, (rows, B_WIDTH), 0) // n_new
    lh = lax.broadcasted_iota(jnp.int32, (rows, B_WIDTH), 1) // HEAD_DIM
    q = q_ref[...]
    qbd = jnp.where(rh == lh, jnp.concatenate([q] * B_HEADS, axis=0), 0.0).astype(BF16)
    lane = lax.broadcasted_iota(jnp.int32, (rows, LANES), 1)

    @pl.when(step == 0)
    def _():
        g_scr[...] = jnp.zeros_like(g_scr)
        m_scr[...] = jnp.zeros_like(m_scr)
        l_scr[...] = jnp.zeros_like(l_scr)

    for bi in range(SAMPLE_BLOCKS_PER_STEP):
        n = step * SAMPLE_BLOCKS_PER_STEP + bi
        s_pages = [_dot(qbd, k_refs[bi * PAGES_PER_BLOCK + pg][...].astype(BF16)) for pg in range(PAGES_PER_BLOCK)]
        raw = s_pages[0]
        mx = s_pages[0]
        for sp in s_pages[1:]:
            raw = raw + sp
            mx = jnp.maximum(mx, sp)
        gate = jnp.sum(raw, axis=1, keepdims=True) * (1.0 / MOBA_BLOCK)
        m_n = jnp.max(mx, axis=1, keepdims=True) * SCALE
        o_all = jnp.zeros((rows, B_WIDTH), F32)
        l_n = jnp.zeros((rows, 1), F32)
        for pg in range(PAGES_PER_BLOCK):
            p = jnp.exp(s_pages[pg] * SCALE - m_n)
            l_n = l_n + jnp.sum(p, axis=1, keepdims=True)
            o_all = o_all + _dot_t(p.astype(BF16), v_refs[bi * PAGES_PER_BLOCK + pg][...].astype(BF16))
        o_scr[n] = _diag_fold(o_all)
        g_scr[...] = jnp.where(lane == n, gate, g_scr[...])
        m_scr[...] = jnp.where(lane == n, m_n, m_scr[...])
        l_scr[...] = jnp.where(lane == n, l_n, l_scr[...])

    @pl.when(step == pl.num_programs(1) - 1)
    def _():
        gate = jnp.where(lane < nb_past, g_scr[...], -jnp.inf)
        sel = jnp.zeros((rows, LANES), jnp.bool_)
        for _ in range(n_sel):
            mx = jnp.max(gate, axis=1, keepdims=True)
            idx = jnp.min(jnp.where(gate == mx, lane, LANES), axis=1, keepdims=True)
            hit = lane == idx
            sel = sel | hit
            gate = jnp.where(hit, -jnp.inf, gate)
        m_blk = m_scr[...]
        s_own = _dot_t(qbd, kn_ref[...].astype(BF16)) * SCALE
        qi = lax.broadcasted_iota(jnp.int32, (rows, n_new), 0) % n_new
        ti = lax.broadcasted_iota(jnp.int32, (rows, n_new), 1)
        s_own = jnp.where(ti <= qi, s_own, NEG)
        m = jnp.maximum(jnp.max(jnp.where(sel, m_blk, NEG), axis=1, keepdims=True),
                        jnp.max(s_own, axis=1, keepdims=True))
        w = jnp.where(sel, jnp.exp(m_blk - m), 0.0)
        p_own = jnp.exp(s_own - m)
        den = jnp.sum(w * l_scr[...], axis=1, keepdims=True) + jnp.sum(p_own, axis=1, keepdims=True)
        num = _diag_fold(_dot(p_own.astype(BF16), vn_ref[...].astype(BF16)))
        for nblk in range(nb_past):
            num = num + w[:, nblk:nblk + 1] * o_scr[nblk]
        out = num / den
        out = jnp.where(rh == lh, jnp.concatenate([out] * (B_WIDTH // LANES), axis=1), 0.0)
        o_ref[...] = jnp.sum(out.reshape(B_HEADS, n_new, B_WIDTH), axis=0)


def _moba_sample(q, k_new, v_new, cache_kt, cache_vt, page_table, past_len):
    db, n_new, _ = q.shape
    assert past_len % MOBA_BLOCK == 0
    nb_past = past_len // MOBA_BLOCK
    assert nb_past % SAMPLE_BLOCKS_PER_STEP == 0 and nb_past <= LANES
    n_sel = min(MOBA_TOPK, nb_past)
    nsteps = nb_past // SAMPLE_BLOCKS_PER_STEP
    npg = SAMPLE_BLOCKS_PER_STEP * PAGES_PER_BLOCK
    rows = B_HEADS * n_new

    def page_map(j, b, s, pt):
        return (pt[b, s * npg + j], 0, 0)

    page_specs = [pl.BlockSpec((None, B_WIDTH, PAGE_SIZE), functools.partial(page_map, j)) for j in range(npg)]
    tok_spec = pl.BlockSpec((None, n_new, B_WIDTH), lambda b, s, pt: (b, 0, 0))
    grid_spec = pltpu.PrefetchScalarGridSpec(
        num_scalar_prefetch=1,
        grid=(db, nsteps),
        in_specs=[tok_spec] * 3 + page_specs + page_specs,
        out_specs=tok_spec,
        scratch_shapes=[pltpu.VMEM((rows, LANES), F32)] * 3 + [pltpu.VMEM((nb_past, rows, LANES), F32)],
    )
    return pl.pallas_call(
        functools.partial(_moba_sample_kernel, n_new, nb_past, n_sel),
        grid_spec=grid_spec,
        out_shape=jax.ShapeDtypeStruct((db, n_new, B_WIDTH), F32),
        compiler_params=pltpu.CompilerParams(dimension_semantics=("arbitrary",) * 2, vmem_limit_bytes=VMEM_LIMIT),
        name="moba_s",
    )(page_table, q, k_new, v_new, *([cache_kt] * npg), *([cache_vt] * npg))


def _merge_kernel(x_ref, pa_ref, gb_ref, yb_ref, wpb_ref, wo_ref, g2_ref, wr_ref, br_ref, h_ref, xn_ref, cmb_ref):
    mix = pa_ref[...] + gb_ref[...] * _dot(yb_ref[...].astype(BF16), wpb_ref[...])
    h = x_ref[...] + _dot(mix.astype(BF16), wo_ref[...])
    h_ref[...] = h
    ms = jnp.mean(h * h, axis=-1, keepdims=True)
    xn = (h * lax.rsqrt(ms + EPS)) * g2_ref[...]
    xn_ref[...] = xn.astype(BF16)
    logits = _dot3(xn, wr_ref[...]) + br_ref[...]
    lane = lax.broadcasted_iota(jnp.int32, logits.shape, 1)
    is_grp = (lane >= N_EXPERTS) & (lane < N_EXPERTS + MOE_GROUPS)
    gl = jnp.where(is_grp, logits, -jnp.inf)
    gmax = jnp.max(gl, axis=1, keepdims=True)
    grp = jnp.min(jnp.where(gl == gmax, lane, LANES), axis=1, keepdims=True) - N_EXPERTS
    p_grp = 1.0 / jnp.sum(jnp.exp(gl - gmax), axis=1, keepdims=True)
    el = jnp.where((lane // MOE_PER_GROUP == grp) & (lane < N_EXPERTS), logits, -jnp.inf)
    v1 = jnp.max(el, axis=1, keepdims=True)
    i1 = jnp.min(jnp.where(el == v1, lane, LANES), axis=1, keepdims=True)
    el2 = jnp.where(lane == i1, -jnp.inf, el)
    v2 = jnp.max(el2, axis=1, keepdims=True)
    i2 = jnp.min(jnp.where(el2 == v2, lane, LANES), axis=1, keepdims=True)
    e2 = jnp.exp(v2 - v1)
    w1 = p_grp / (1.0 + e2)
    w2 = p_grp * e2 / (1.0 + e2)
    cmb_ref[...] = jnp.where(lane == i1, w1, 0.0) + jnp.where(lane == i2, w2, 0.0)


def _merge(x2d, pa, gb, yb, prm):
    n = x2d.shape[0]
    nt = n // ROW_TILE
    full = lambda a: pl.BlockSpec(a.shape, lambda i: (0,) * a.ndim)
    rows = lambda w: pl.BlockSpec((ROW_TILE, w), lambda i: (i, 0))
    consts = [prm['wpb'], prm['wo'], prm['g2'], prm['wr'], prm['br']]
    return pl.pallas_call(
        _merge_kernel,
        grid=(nt,),
        in_specs=[rows(D_MODEL), rows(D_MODEL), rows(D_MODEL), rows(B_WIDTH)] + [full(a) for a in consts],
        out_specs=[rows(D_MODEL), rows(D_MODEL), rows(LANES)],
        out_shape=[jax.ShapeDtypeStruct((n, D_MODEL), F32), jax.ShapeDtypeStruct((n, D_MODEL), BF16),
                   jax.ShapeDtypeStruct((n, LANES), F32)],
        compiler_params=pltpu.CompilerParams(dimension_semantics=("arbitrary",), vmem_limit_bytes=VMEM_LIMIT),
        name="merge",
    )(x2d, pa, gb, yb, *consts)


def _moe_kernel(xn_ref, h_ref, cmb_ref, w1_ref, w3_ref, w2_ref, o_ref):
    e = pl.program_id(1)

    @pl.when(e == 0)
    def _():
        o_ref[...] = h_ref[...]

    x = xn_ref[...]
    hid = jax.nn.silu(_dot(x, w1_ref[...])) * _dot(x, w3_ref[...])
    cmb = cmb_ref[...]
    lane = lax.broadcasted_iota(jnp.int32, cmb.shape, 1)
    col = jnp.sum(jnp.where(lane == e, cmb, 0.0), axis=1, keepdims=True)
    o_ref[...] += _dot((hid * col).astype(BF16), w2_ref[...])


def _moe(xn, h, cmb, prm):
    n = xn.shape[0]
    tm = min(MOE_ROW_TILE, n)
    assert n % tm == 0
    rows = lambda w: pl.BlockSpec((tm, w), lambda i, e: (i, 0))
    return pl.pallas_call(
        _moe_kernel,
        grid=(n // tm, N_EXPERTS),
        in_specs=[rows(D_MODEL), rows(D_MODEL), rows(LANES),
                  pl.BlockSpec((None, D_MODEL, D_EXPERT), lambda i, e: (e, 0, 0)),
                  pl.BlockSpec((None, D_MODEL, D_EXPERT), lambda i, e: (e, 0, 0)),
                  pl.BlockSpec((None, D_EXPERT, D_MODEL), lambda i, e: (e, 0, 0))],
        out_specs=rows(D_MODEL),
        out_shape=jax.ShapeDtypeStruct((n, D_MODEL), F32),
        compiler_params=pltpu.CompilerParams(dimension_semantics=("arbitrary",) * 2, vmem_limit_bytes=VMEM_LIMIT),
        name="moe",
    )(xn, h, cmb, prm['w1'], prm['w3'], prm['w2'])


def _prepare(norm1_g, w_in, b_gate, ln_v_g, ln_v_b, qn_g, kn_g, w_pa, w_pb, w_o, norm2_g,
             w_rg, b_rg, w_re, b_re, w1, w3, w2):
    seg = jnp.kron(jnp.eye(B_HEADS, dtype=F32), jnp.full((HEAD_DIM, HEAD_DIM), 1.0 / HEAD_DIM, F32))
    pad = LANES - N_EXPERTS - MOE_GROUPS
    return dict(
        g1=norm1_g.reshape(1, D_MODEL), win=w_in.astype(BF16), bgate=b_gate.reshape(1, 2 * D_MODEL),
        lng=ln_v_g.reshape(1, A_WIDTH), lnb=ln_v_b.reshape(1, A_WIDTH), seg=seg.astype(BF16),
        qg=jnp.tile(qn_g, B_HEADS).reshape(1, B_WIDTH), kg=jnp.tile(kn_g, B_HEADS).reshape(1, B_WIDTH),
        wpa=w_pa.astype(BF16), wpb=w_pb.astype(BF16), wo=w_o.astype(BF16), g2=norm2_g.reshape(1, D_MODEL),
        wr=jnp.pad(jnp.concatenate([w_re, w_rg], axis=1), ((0, 0), (0, pad))),
        br=jnp.pad(jnp.concatenate([b_re, b_rg]), (0, pad)).reshape(1, LANES),
        w1=w1.astype(BF16), w3=w3.astype(BF16), w2=w2.astype(BF16),
    )


def _spatial_weights(w_sp, b_sp, length):
    reps = CHUNK // length
    tril = jnp.tril(jnp.ones((length, length), F32))
    w = w_sp[:, :length, :length] * tril
    w = jax.vmap(lambda m: jnp.kron(jnp.eye(reps, dtype=F32), m))(w)
    bias = jnp.tile(b_sp[:, :length].T, (reps, 1))
    return w.reshape(A_GROUPS * CHUNK, CHUNK).astype(BF16), jnp.repeat(bias, A_GROUP_DIM, axis=1)


def kernel(x_prompt, x_sample, cache_k, cache_v, page_table, norm1_g, w_in, b_gate, ln_v_g, ln_v_b, w_sp, b_sp,
           qn_g, kn_g, w_pa, w_pb, w_o, norm2_g, w_rg, b_rg, w_re, b_re, w1, w3, w2):
    bsz, seq, _ = x_prompt.shape
    db, n_new, _ = x_sample.shape
    past_len = page_table.shape[1] * PAGE_SIZE
    prm = _prepare(norm1_g, w_in, b_gate, ln_v_g, ln_v_b, qn_g, kn_g, w_pa, w_pb, w_o, norm2_g,
                   w_rg, b_rg, w_re, b_re, w1, w3, w2)

    xp = x_prompt.reshape(bsz * seq, D_MODEL)
    wst, bsp = _spatial_weights(w_sp, b_sp, CHUNK)
    pa, gb, k_p, v_p, qt, vt, kmean = _mixer(xp, prm, wst, bsp, True)
    yb = _moba_prompt(qt, k_p, vt, kmean.reshape(-1, B_WIDTH), bsz, seq)
    h, xn, cmb = _merge(xp, pa, gb, yb, prm)
    y_p = _moe(xn, h, cmb, prm)

    xs = x_sample.reshape(db * n_new, D_MODEL)
    wst, bsp = _spatial_weights(w_sp, b_sp, n_new)
    pa, gb, q_s, k_s, v_s, va_s = _mixer(xs, prm, wst, bsp, False)
    n_phys = cache_k.shape[0]
    ckt = jnp.transpose(cache_k, (0, 2, 3, 1)).reshape(n_phys, B_WIDTH, PAGE_SIZE)
    cvt = jnp.transpose(cache_v, (0, 2, 3, 1)).reshape(n_phys, B_WIDTH, PAGE_SIZE)
    tok3 = lambda a: a.reshape(db, n_new, B_WIDTH)
    yb = _moba_sample(tok3(q_s), tok3(k_s), tok3(v_s), ckt, cvt, page_table, past_len)
    h, xn, cmb = _merge(xs, pa, gb, yb.reshape(db * n_new, B_WIDTH), prm)
    y_s = _moe(xn, h, cmb, prm)

    heads = lambda a, lead: a.reshape(*lead, B_HEADS, HEAD_DIM)
    return (y_p.reshape(bsz, seq, D_MODEL), y_s.reshape(db, n_new, D_MODEL),
            heads(k_p, (bsz, seq)), heads(v_p, (bsz, seq)),
            heads(k_s, (db, n_new)), heads(v_s, (db, n_new)),
            va_s.reshape(db, n_new, A_WIDTH))
```

```python
import functools

import jax
import jax.numpy as jnp
from jax import lax
from jax.experimental import pallas as pl
from jax.experimental.pallas import tpu as pltpu

F32 = jnp.float32
BF16 = jnp.bfloat16

D_MODEL = 1024
A_GROUPS = 8
A_GROUP_DIM = 64
A_WIDTH = A_GROUPS * A_GROUP_DIM
CHUNK = 128
B_HEADS = 8
HEAD_DIM = 64
B_WIDTH = B_HEADS * HEAD_DIM
MOBA_BLOCK = 256
MOBA_TOPK = 3
PAGE_SIZE = 128
SCALE = HEAD_DIM ** -0.5
MOE_GROUPS = 4
MOE_PER_GROUP = 4
N_EXPERTS = MOE_GROUPS * MOE_PER_GROUP
D_EXPERT = 512
EPS = 1e-6
OFF_U = 0
OFF_VA = OFF_U + A_WIDTH
OFF_Q = OFF_VA + A_WIDTH
OFF_K = OFF_Q + B_WIDTH
OFF_VB = OFF_K + B_WIDTH
OFF_GATE = OFF_VB + B_WIDTH

LANES = 128
ROW_TILE = 256
MOE_ROW_TILE = 1024
PAIR = 2 * HEAD_DIM
VMEM_LIMIT = 56 * 1024 * 1024
NEG = -1e30


def _dot(a, b):
    return jnp.dot(a, b, preferred_element_type=F32)


def _dot_t(a, b):
    return lax.dot_general(a, b, (((1,), (1,)), ((), ())), preferred_element_type=F32)


def _split(a):
    hi = a.astype(BF16)
    lo = (a - hi.astype(F32)).astype(BF16)
    return hi, lo


def _dot3(a, b):
    ah, al = _split(a)
    bh, bl = _split(b)
    return _dot(ah, bh) + _dot(ah, bl) + _dot(al, bh)


def _mixer_kernel(prompt, x_ref, g1_ref, win_ref, bgate_ref, lng_ref, lnb_ref, seg_ref, wst_ref, bsp_ref,
                  qg_ref, kg_ref, wpa_ref, *outs):
    if prompt:
        pa_ref, gb_ref, k_ref, qt_ref, kt_ref, vt_ref, km_ref = outs
    else:
        pa_ref, gb_ref, q_ref, k_ref, v_ref, va_ref = outs
    x = x_ref[...]
    ms = jnp.mean(x * x, axis=-1, keepdims=True)
    xb = ((x * lax.rsqrt(ms + EPS)) * g1_ref[...]).astype(BF16)
    seg = seg_ref[...]

    def proj(off, width):
        return _dot(xb, win_ref[:, off:off + width])

    def segmean(a):
        return _dot(a.astype(BF16), seg)

    u = jax.nn.gelu(proj(OFF_U, A_WIDTH))
    va = jax.nn.gelu(proj(OFF_VA, A_WIDTH))
    xc = va - segmean(va)
    va = xc * lax.rsqrt(segmean(xc * xc) + EPS) * lng_ref[...] + lnb_ref[...]
    if not prompt:
        va_ref[...] = va

    wst = wst_ref[...]
    grp = lax.broadcasted_iota(jnp.int32, (CHUNK, A_WIDTH), 1) // A_GROUP_DIM
    ya = []
    for c in range(ROW_TILE // CHUNK):
        rows = slice(c * CHUNK, (c + 1) * CHUNK)
        r = _dot(wst, va[rows].astype(BF16))
        mixed = bsp_ref[...]
        for g in range(A_GROUPS):
            mixed = mixed + jnp.where(grp == g, r[g * CHUNK:(g + 1) * CHUNK], 0.0)
        ya.append(u[rows] * mixed)
    ya = jnp.concatenate(ya, axis=0)

    gates_a = jax.nn.sigmoid(proj(OFF_GATE, D_MODEL) + bgate_ref[:, :D_MODEL])
    pa_ref[...] = gates_a * _dot(ya.astype(BF16), wpa_ref[...])
    gb_ref[...] = jax.nn.sigmoid(proj(OFF_GATE + D_MODEL, D_MODEL) + bgate_ref[:, D_MODEL:])

    zq = proj(OFF_Q, B_WIDTH)
    q = zq * lax.rsqrt(segmean(zq * zq) + EPS) * qg_ref[...]
    zk = proj(OFF_K, B_WIDTH)
    k = zk * lax.rsqrt(segmean(zk * zk) + EPS) * kg_ref[...]
    v = proj(OFF_VB, B_WIDTH)
    k_ref[...] = k
    if prompt:
        qt_ref[...] = q.T
        kt_ref[...] = k.T
        vt_ref[...] = v.T
        km_ref[0] = jnp.mean(k, axis=0, keepdims=True)
    else:
        v_ref[...] = v
        q_ref[...] = q


def _mixer(x2d, prm, wst, bsp, prompt, seq=None):
    n = x2d.shape[0]
    assert n % ROW_TILE == 0
    nt = n // ROW_TILE
    full = lambda a: pl.BlockSpec(a.shape, lambda i: (0,) * a.ndim)
    rows = lambda w: pl.BlockSpec((ROW_TILE, w), lambda i: (i, 0))
    consts = [prm['g1'], prm['win'], prm['bgate'], prm['lng'], prm['lnb'], prm['seg'], wst, bsp,
              prm['qg'], prm['kg'], prm['wpa']]
    out_shape = [jax.ShapeDtypeStruct((n, D_MODEL), F32), jax.ShapeDtypeStruct((n, D_MODEL), F32)]
    out_specs = [rows(D_MODEL), rows(D_MODEL)]
    if prompt:
        tps = seq // ROW_TILE
        out_shape += [jax.ShapeDtypeStruct((n, B_WIDTH), F32)]
        out_specs += [rows(B_WIDTH)]
        out_shape += [jax.ShapeDtypeStruct((n // seq, B_WIDTH, seq), F32)] * 3
        out_specs += [pl.BlockSpec((None, B_WIDTH, ROW_TILE), lambda i: (i // tps, 0, i % tps))] * 3
        out_shape += [jax.ShapeDtypeStruct((nt, 1, B_WIDTH), F32)]
        out_specs += [pl.BlockSpec((1, 1, B_WIDTH), lambda i: (i, 0, 0))]
    else:
        out_shape += [jax.ShapeDtypeStruct((n, B_WIDTH), F32)] * 3
        out_specs += [rows(B_WIDTH)] * 3
        out_shape += [jax.ShapeDtypeStruct((n, A_WIDTH), F32)]
        out_specs += [rows(A_WIDTH)]
    return pl.pallas_call(
        functools.partial(_mixer_kernel, prompt),
        grid=(nt,),
        in_specs=[rows(D_MODEL)] + [full(a) for a in consts],
        out_specs=out_specs,
        out_shape=out_shape,
        compiler_params=pltpu.CompilerParams(dimension_semantics=("arbitrary",), vmem_limit_bytes=VMEM_LIMIT),
        name="mixer_p" if prompt else "mixer_s",
    )(x2d, *consts)


LOG2E = 1.4426950408889634


def _moba_prompt_kernel(n_sel, qt_ref, k_ref, vt_ref, km_ref, o_ref, bias_ref):
    i = pl.program_id(2)
    blk = MOBA_BLOCK
    nb = km_ref.shape[0]
    qt2 = qt_ref[...]
    km2 = km_ref[...]
    row = lax.broadcasted_iota(jnp.int32, (PAIR, blk), 0)
    brow = lax.broadcasted_iota(jnp.int32, (nb, blk), 0)
    hrows = [slice(hh * HEAD_DIM, (hh + 1) * HEAD_DIM) for hh in range(2)]
    qs = []
    for hh in range(2):
        qm = jnp.where((row // HEAD_DIM) == hh, qt2, 0.0)
        gate = jnp.where(brow < i, _dot3(km2, qm), -jnp.inf)
        bias = jnp.full((nb, blk), NEG, F32)
        for r in range(n_sel):
            mx = jnp.max(gate, axis=0, keepdims=True)
            idx = jnp.min(jnp.where(gate == mx, brow, nb), axis=0, keepdims=True)
            hit = brow == idx
            bias = jnp.where(hit & (r < i), 0.0, bias)
            gate = jnp.where(hit, -jnp.inf, gate)
        bias_ref[hh] = bias
        qs.append((qm * (SCALE * LOG2E)).astype(BF16))

    own = pl.multiple_of(i * blk, blk)
    kb = k_ref[pl.ds(own, blk), :].astype(BF16)
    kpos = lax.broadcasted_iota(jnp.int32, (blk, blk), 0)
    qpos = lax.broadcasted_iota(jnp.int32, (blk, blk), 1)
    s_own = [_dot(kb, qs[hh]) for hh in range(2)]
    state = []
    for hh in range(2):
        s = jnp.where(kpos <= qpos, s_own[hh], NEG)
        m = jnp.max(s, axis=0, keepdims=True)
        p = jnp.exp2(s - m)
        l = jnp.sum(p, axis=0, keepdims=True)
        acc = _dot(vt_ref[hrows[hh], pl.ds(own, blk)].astype(BF16), p.astype(BF16))
        state.append((m, l, acc))

    def body(c, carry):
        off = pl.multiple_of(c * (2 * blk), 2 * blk)
        kc = k_ref[pl.ds(off, 2 * blk), :].astype(BF16)
        s2 = [_dot(kc, qs[hh]) for hh in range(2)]
        soft = []
        for hh in range(2):
            m, l, acc = carry[hh]
            s_lo = s2[hh][:blk] + bias_ref[hh, pl.ds(2 * c, 1), :]
            s_hi = s2[hh][blk:] + bias_ref[hh, pl.ds(2 * c + 1, 1), :]
            m_new = jnp.maximum(m, jnp.maximum(jnp.max(s_lo, axis=0, keepdims=True),
                                               jnp.max(s_hi, axis=0, keepdims=True)))
            alpha = jnp.exp2(m - m_new)
            p_lo = jnp.exp2(s_lo - m_new)
            p_hi = jnp.exp2(s_hi - m_new)
            l = alpha * l + jnp.sum(p_lo, axis=0, keepdims=True) + jnp.sum(p_hi, axis=0, keepdims=True)
            soft.append((m_new, l, alpha, jnp.concatenate([p_lo, p_hi], axis=0).astype(BF16)))
        new = []
        for hh in range(2):
            m_new, l, alpha, p = soft[hh]
            acc = alpha * carry[hh][2] + _dot(vt_ref[hrows[hh], pl.ds(off, 2 * blk)].astype(BF16), p)
            new.append((m_new, l, acc))
        return tuple(new)

    state = lax.fori_loop(0, (i + 1) // 2, body, tuple(state))
    o_ref[...] = jnp.concatenate([acc / l for (_, l, acc) in state], axis=0).T


def _moba_prompt(qt, k, vt, kmean, bsz, seq):
    nb = seq // MOBA_BLOCK
    assert nb % 2 == 0
    n_sel = min(MOBA_TOPK, nb - 1)
    npair = B_WIDTH // PAIR
    return pl.pallas_call(
        functools.partial(_moba_prompt_kernel, n_sel),
        grid=(bsz, npair, nb),
        in_specs=[
            pl.BlockSpec((None, PAIR, MOBA_BLOCK), lambda b, p, i: (b, p, i)),
            pl.BlockSpec((seq, PAIR), lambda b, p, i: (b, p)),
            pl.BlockSpec((None, PAIR, seq), lambda b, p, i: (b, p, 0)),
            pl.BlockSpec((nb, PAIR), lambda b, p, i: (b, p)),
        ],
        out_specs=pl.BlockSpec((MOBA_BLOCK, PAIR), lambda b, p, i: (b * nb + i, p)),
        out_shape=jax.ShapeDtypeStruct((bsz * seq, B_WIDTH), F32),
        scratch_shapes=[pltpu.VMEM((2, nb, MOBA_BLOCK), F32)],
        compiler_params=pltpu.CompilerParams(dimension_semantics=("arbitrary",) * 3, vmem_limit_bytes=VMEM_LIMIT),
        name="moba_p",
    )(qt, k, vt, kmean)


SAMPLE_BLOCKS_PER_STEP = 4
PAGES_PER_BLOCK = MOBA_BLOCK // PAGE_SIZE


def _diag_fold(o_all):
    rows = o_all.shape[0]
    n_new = rows // B_HEADS
    rh = lax.broadcasted_iota(jnp.int32, (rows, B_WIDTH), 0) // n_new
    lh = lax.broadcasted_iota(jnp.int32, (rows, B_WIDTH), 1) // HEAD_DIM
    o = jnp.where(rh == lh, o_all, 0.0)
    out = o[:, 0:LANES]
    for t in range(1, B_WIDTH // LANES):
        out = out + o[:, t * LANES:(t + 1) * LANES]
    return out


def _moba_sample_kernel(n_new, nb_past, n_sel, pt_ref, q_ref, kn_ref, vn_ref, *rest):
    npg = SAMPLE_BLOCKS_PER_STEP * PAGES_PER_BLOCK
    k_refs = rest[:npg]
    v_refs = rest[npg:2 * npg]
    o_ref = rest[2 * npg]
    g_scr, m_scr, l_scr, o_scr = rest[2 * npg + 1:]
    step = pl.program_id(1)
    rows = B_HEADS * n_new
    rh = lax.broadcasted_iota(jnp.int32, (rows, B_WIDTH), 0) // n_new
    lh = lax.broadcasted_iota(jnp.int32, (rows, B_WIDTH), 1) // HEAD_DIM
    q = q_ref[...]
    qbd = jnp.where(rh == lh, jnp.concatenate([q] * B_HEADS, axis=0), 0.0).astype(BF16)
    lane = lax.broadcasted_iota(jnp.int32, (rows, LANES), 1)

    @pl.when(step == 0)
    def _():
        g_scr[...] = jnp.zeros_like(g_scr)
        m_scr[...] = jnp.zeros_like(m_scr)
        l_scr[...] = jnp.zeros_like(l_scr)

    s_pages = [_dot(qbd, k_refs[j][...].astype(BF16)) for j in range(npg)]
    probs, gates, maxes, sums = [], [], [], []
    for bi in range(SAMPLE_BLOCKS_PER_STEP):
        blk_pages = s_pages[bi * PAGES_PER_BLOCK:(bi + 1) * PAGES_PER_BLOCK]
        raw = blk_pages[0]
        mx = blk_pages[0]
        for sp in blk_pages[1:]:
            raw = raw + sp
            mx = jnp.maximum(mx, sp)
        gates.append(jnp.sum(raw, axis=1, keepdims=True) * (1.0 / MOBA_BLOCK))
        m_n = jnp.max(mx, axis=1, keepdims=True) * SCALE
        l_n = jnp.zeros((rows, 1), F32)
        for sp in blk_pages:
            p = jnp.exp(sp * SCALE - m_n)
            l_n = l_n + jnp.sum(p, axis=1, keepdims=True)
            probs.append(p.astype(BF16))
        maxes.append(m_n)
        sums.append(l_n)
    outs = []
    for bi in range(SAMPLE_BLOCKS_PER_STEP):
        o_all = _dot_t(probs[bi * PAGES_PER_BLOCK], v_refs[bi * PAGES_PER_BLOCK][...].astype(BF16))
        for pg in range(1, PAGES_PER_BLOCK):
            j = bi * PAGES_PER_BLOCK + pg
            o_all = o_all + _dot_t(probs[j], v_refs[j][...].astype(BF16))
        outs.append(o_all)
    g_new, m_new, l_new = g_scr[...], m_scr[...], l_scr[...]
    for bi in range(SAMPLE_BLOCKS_PER_STEP):
        n = step * SAMPLE_BLOCKS_PER_STEP + bi
        o_scr[n] = _diag_fold(outs[bi])
        g_new = jnp.where(lane == n, gates[bi], g_new)
        m_new = jnp.where(lane == n, maxes[bi], m_new)
        l_new = jnp.where(lane == n, sums[bi], l_new)
    g_scr[...] = g_new
    m_scr[...] = m_new
    l_scr[...] = l_new

    @pl.when(step == pl.num_programs(1) - 1)
    def _():
        gate = jnp.where(lane < nb_past, g_scr[...], -jnp.inf)
        sel = jnp.zeros((rows, LANES), jnp.bool_)
        for _ in range(n_sel):
            mx = jnp.max(gate, axis=1, keepdims=True)
            idx = jnp.min(jnp.where(gate == mx, lane, LANES), axis=1, keepdims=True)
            hit = lane == idx
            sel = sel | hit
            gate = jnp.where(hit, -jnp.inf, gate)
        m_blk = m_scr[...]
        s_own = _dot_t(qbd, kn_ref[...].astype(BF16)) * SCALE
        qi = lax.broadcasted_iota(jnp.int32, (rows, n_new), 0) % n_new
        ti = lax.broadcasted_iota(jnp.int32, (rows, n_new), 1)
        s_own = jnp.where(ti <= qi, s_own, NEG)
        m = jnp.maximum(jnp.max(jnp.where(sel, m_blk, NEG), axis=1, keepdims=True),
                        jnp.max(s_own, axis=1, keepdims=True))
        w = jnp.where(sel, jnp.exp(m_blk - m), 0.0)
        p_own = jnp.exp(s_own - m)
        den = jnp.sum(w * l_scr[...], axis=1, keepdims=True) + jnp.sum(p_own, axis=1, keepdims=True)
        num = _diag_fold(_dot(p_own.astype(BF16), vn_ref[...].astype(BF16)))
        for nblk in range(nb_past):
            num = num + w[:, nblk:nblk + 1] * o_scr[nblk]
        out = num / den
        out = jnp.where(rh == lh, jnp.concatenate([out] * (B_WIDTH // LANES), axis=1), 0.0)
        o_ref[...] = jnp.sum(out.reshape(B_HEADS, n_new, B_WIDTH), axis=0)


def _moba_sample(q, k_new, v_new, cache_kt, cache_vt, page_table, past_len):
    db, n_new, _ = q.shape
    assert past_len % MOBA_BLOCK == 0
    nb_past = past_len // MOBA_BLOCK
    assert nb_past % SAMPLE_BLOCKS_PER_STEP == 0 and nb_past <= LANES
    n_sel = min(MOBA_TOPK, nb_past)
    nsteps = nb_past // SAMPLE_BLOCKS_PER_STEP
    npg = SAMPLE_BLOCKS_PER_STEP * PAGES_PER_BLOCK
    rows = B_HEADS * n_new

    def page_map(j, b, s, pt):
        return (pt[b, s * npg + j], 0, 0)

    page_specs = [pl.BlockSpec((None, B_WIDTH, PAGE_SIZE), functools.partial(page_map, j)) for j in range(npg)]
    tok_spec = pl.BlockSpec((None, n_new, B_WIDTH), lambda b, s, pt: (b, 0, 0))
    grid_spec = pltpu.PrefetchScalarGridSpec(
        num_scalar_prefetch=1,
        grid=(db, nsteps),
        in_specs=[tok_spec] * 3 + page_specs + page_specs,
        out_specs=tok_spec,
        scratch_shapes=[pltpu.VMEM((rows, LANES), F32)] * 3 + [pltpu.VMEM((nb_past, rows, LANES), F32)],
    )
    return pl.pallas_call(
        functools.partial(_moba_sample_kernel, n_new, nb_past, n_sel),
        grid_spec=grid_spec,
        out_shape=jax.ShapeDtypeStruct((db, n_new, B_WIDTH), F32),
        compiler_params=pltpu.CompilerParams(dimension_semantics=("arbitrary",) * 2, vmem_limit_bytes=VMEM_LIMIT),
        name="moba_s",
    )(page_table, q, k_new, v_new, *([cache_kt] * npg), *([cache_vt] * npg))


def _merge_kernel(x_ref, pa_ref, gb_ref, yb_ref, wpb_ref, wo_ref, g2_ref, wr_ref, br_ref, h_ref, xn_ref, cmb_ref):
    mix = pa_ref[...] + gb_ref[...] * _dot(yb_ref[...].astype(BF16), wpb_ref[...])
    h = x_ref[...] + _dot(mix.astype(BF16), wo_ref[...])
    h_ref[...] = h
    ms = jnp.mean(h * h, axis=-1, keepdims=True)
    xn = (h * lax.rsqrt(ms + EPS)) * g2_ref[...]
    xn_ref[...] = xn.astype(BF16)
    logits = _dot3(xn, wr_ref[...]) + br_ref[...]
    lane = lax.broadcasted_iota(jnp.int32, logits.shape, 1)
    is_grp = (lane >= N_EXPERTS) & (lane < N_EXPERTS + MOE_GROUPS)
    gl = jnp.where(is_grp, logits, -jnp.inf)
    gmax = jnp.max(gl, axis=1, keepdims=True)
    grp = jnp.min(jnp.where(gl == gmax, lane, LANES), axis=1, keepdims=True) - N_EXPERTS
    p_grp = 1.0 / jnp.sum(jnp.exp(gl - gmax), axis=1, keepdims=True)
    el = jnp.where((lane // MOE_PER_GROUP == grp) & (lane < N_EXPERTS), logits, -jnp.inf)
    v1 = jnp.max(el, axis=1, keepdims=True)
    i1 = jnp.min(jnp.where(el == v1, lane, LANES), axis=1, keepdims=True)
    el2 = jnp.where(lane == i1, -jnp.inf, el)
    v2 = jnp.max(el2, axis=1, keepdims=True)
    i2 = jnp.min(jnp.where(el2 == v2, lane, LANES), axis=1, keepdims=True)
    e2 = jnp.exp(v2 - v1)
    w1 = p_grp / (1.0 + e2)
    w2 = p_grp * e2 / (1.0 + e2)
    cmb_ref[...] = jnp.where(lane == i1, w1, 0.0) + jnp.where(lane == i2, w2, 0.0)


def _merge(x2d, pa, gb, yb, prm):
    n = x2d.shape[0]
    nt = n // ROW_TILE
    full = lambda a: pl.BlockSpec(a.shape, lambda i: (0,) * a.ndim)
    rows = lambda w: pl.BlockSpec((ROW_TILE, w), lambda i: (i, 0))
    consts = [prm['wpb'], prm['wo'], prm['g2'], prm['wr'], prm['br']]
    return pl.pallas_call(
        _merge_kernel,
        grid=(nt,),
        in_specs=[rows(D_MODEL), rows(D_MODEL), rows(D_MODEL), rows(B_WIDTH)] + [full(a) for a in consts],
        out_specs=[rows(D_MODEL), rows(D_MODEL), rows(LANES)],
        out_shape=[jax.ShapeDtypeStruct((n, D_MODEL), F32), jax.ShapeDtypeStruct((n, D_MODEL), BF16),
                   jax.ShapeDtypeStruct((n, LANES), F32)],
        compiler_params=pltpu.CompilerParams(dimension_semantics=("arbitrary",), vmem_limit_bytes=VMEM_LIMIT),
        name="merge",
    )(x2d, pa, gb, yb, *consts)


def _moe_kernel(xn_ref, h_ref, cmb_ref, w1_ref, w3_ref, w2_ref, o_ref):
    e = pl.program_id(1)

    @pl.when(e == 0)
    def _():
        o_ref[...] = h_ref[...]

    x = xn_ref[...]
    hid = jax.nn.silu(_dot(x, w1_ref[...])) * _dot(x, w3_ref[...])
    cmb = cmb_ref[...]
    lane = lax.broadcasted_iota(jnp.int32, cmb.shape, 1)
    col = jnp.sum(jnp.where(lane == e, cmb, 0.0), axis=1, keepdims=True)
    o_ref[...] += _dot((hid * col).astype(BF16), w2_ref[...])


def _moe(xn, h, cmb, prm):
    n = xn.shape[0]
    tm = min(MOE_ROW_TILE, n)
    assert n % tm == 0
    rows = lambda w: pl.BlockSpec((tm, w), lambda i, e: (i, 0))
    return pl.pallas_call(
        _moe_kernel,
        grid=(n // tm, N_EXPERTS),
        in_specs=[rows(D_MODEL), rows(D_MODEL), rows(LANES),
                  pl.BlockSpec((None, D_MODEL, D_EXPERT), lambda i, e: (e, 0, 0)),
                  pl.BlockSpec((None, D_MODEL, D_EXPERT), lambda i, e: (e, 0, 0)),
                  pl.BlockSpec((None, D_EXPERT, D_MODEL), lambda i, e: (e, 0, 0))],
        out_specs=rows(D_MODEL),
        out_shape=jax.ShapeDtypeStruct((n, D_MODEL), F32),
        compiler_params=pltpu.CompilerParams(dimension_semantics=("arbitrary",) * 2, vmem_limit_bytes=VMEM_LIMIT),
        name="moe",
    )(xn, h, cmb, prm['w1'], prm['w3'], prm['w2'])


def _prepare(norm1_g, w_in, b_gate, ln_v_g, ln_v_b, qn_g, kn_g, w_pa, w_pb, w_o, norm2_g,
             w_rg, b_rg, w_re, b_re, w1, w3, w2):
    seg = jnp.kron(jnp.eye(B_HEADS, dtype=F32), jnp.full((HEAD_DIM, HEAD_DIM), 1.0 / HEAD_DIM, F32))
    pad = LANES - N_EXPERTS - MOE_GROUPS
    return dict(
        g1=norm1_g.reshape(1, D_MODEL), win=w_in.astype(BF16), bgate=b_gate.reshape(1, 2 * D_MODEL),
        lng=ln_v_g.reshape(1, A_WIDTH), lnb=ln_v_b.reshape(1, A_WIDTH), seg=seg.astype(BF16),
        qg=jnp.tile(qn_g, B_HEADS).reshape(1, B_WIDTH), kg=jnp.tile(kn_g, B_HEADS).reshape(1, B_WIDTH),
        wpa=w_pa.astype(BF16), wpb=w_pb.astype(BF16), wo=w_o.astype(BF16), g2=norm2_g.reshape(1, D_MODEL),
        wr=jnp.pad(jnp.concatenate([w_re, w_rg], axis=1), ((0, 0), (0, pad))),
        br=jnp.pad(jnp.concatenate([b_re, b_rg]), (0, pad)).reshape(1, LANES),
        w1=w1.astype(BF16), w3=w3.astype(BF16), w2=w2.astype(BF16),
    )


def _spatial_weights(w_sp, b_sp, length):
    reps = CHUNK // length
    tril = jnp.tril(jnp.ones((length, length), F32))
    w = w_sp[:, :length, :length] * tril
    w = jax.vmap(lambda m: jnp.kron(jnp.eye(reps, dtype=F32), m))(w)
    bias = jnp.tile(b_sp[:, :length].T, (reps, 1))
    return w.reshape(A_GROUPS * CHUNK, CHUNK).astype(BF16), jnp.repeat(bias, A_GROUP_DIM, axis=1)


def kernel(x_prompt, x_sample, cache_k, cache_v, page_table, norm1_g, w_in, b_gate, ln_v_g, ln_v_b, w_sp, b_sp,
           qn_g, kn_g, w_pa, w_pb, w_o, norm2_g, w_rg, b_rg, w_re, b_re, w1, w3, w2):
    bsz, seq, _ = x_prompt.shape
    db, n_new, _ = x_sample.shape
    past_len = page_table.shape[1] * PAGE_SIZE
    prm = _prepare(norm1_g, w_in, b_gate, ln_v_g, ln_v_b, qn_g, kn_g, w_pa, w_pb, w_o, norm2_g,
                   w_rg, b_rg, w_re, b_re, w1, w3, w2)

    xp = x_prompt.reshape(bsz * seq, D_MODEL)
    wst, bsp = _spatial_weights(w_sp, b_sp, CHUNK)
    pa, gb, k_p, qt, kt, vt, kmean = _mixer(xp, prm, wst, bsp, True, seq)
    yb = _moba_prompt(qt, k_p, vt, kmean.reshape(-1, B_WIDTH), bsz, seq)
    h, xn, cmb = _merge(xp, pa, gb, yb, prm)
    y_p = _moe(xn, h, cmb, prm)

    xs = x_sample.reshape(db * n_new, D_MODEL)
    wst, bsp = _spatial_weights(w_sp, b_sp, n_new)
    pa, gb, q_s, k_s, v_s, va_s = _mixer(xs, prm, wst, bsp, False)
    n_phys = cache_k.shape[0]
    ckt = jnp.transpose(cache_k, (0, 2, 3, 1)).reshape(n_phys, B_WIDTH, PAGE_SIZE)
    cvt = jnp.transpose(cache_v, (0, 2, 3, 1)).reshape(n_phys, B_WIDTH, PAGE_SIZE)
    tok3 = lambda a: a.reshape(db, n_new, B_WIDTH)
    yb = _moba_sample(tok3(q_s), tok3(k_s), tok3(v_s), ckt, cvt, page_table, past_len)
    h, xn, cmb = _merge(xs, pa, gb, yb.reshape(db * n_new, B_WIDTH), prm)
    y_s = _moe(xn, h, cmb, prm)

    heads = lambda a, lead: a.reshape(*lead, B_HEADS, HEAD_DIM)
    from_t = lambda a: jnp.transpose(a.reshape(bsz, B_HEADS, HEAD_DIM, seq), (0, 3, 1, 2))
    return (y_p.reshape(bsz, seq, D_MODEL), y_s.reshape(db, n_new, D_MODEL),
            from_t(kt), from_t(vt),
            heads(k_s, (db, n_new)), heads(v_s, (db, n_new)),
            va_s.reshape(db, n_new, A_WIDTH))
```

```python
import functools

import jax
import jax.numpy as jnp
from jax import lax
from jax.experimental import pallas as pl
from jax.experimental.pallas import tpu as pltpu

F32 = jnp.float32
BF16 = jnp.bfloat16

D_MODEL = 1024
A_GROUPS = 8
A_GROUP_DIM = 64
A_WIDTH = A_GROUPS * A_GROUP_DIM
CHUNK = 128
B_HEADS = 8
HEAD_DIM = 64
B_WIDTH = B_HEADS * HEAD_DIM
MOBA_BLOCK = 256
MOBA_TOPK = 3
PAGE_SIZE = 128
SCALE = HEAD_DIM ** -0.5
MOE_GROUPS = 4
MOE_PER_GROUP = 4
N_EXPERTS = MOE_GROUPS * MOE_PER_GROUP
D_EXPERT = 512
EPS = 1e-6
OFF_U = 0
OFF_VA = OFF_U + A_WIDTH
OFF_Q = OFF_VA + A_WIDTH
OFF_K = OFF_Q + B_WIDTH
OFF_VB = OFF_K + B_WIDTH
OFF_GATE = OFF_VB + B_WIDTH

LANES = 128
ROW_TILE = 256
MOE_ROW_TILE = 1024
PAIR = 2 * HEAD_DIM
VMEM_LIMIT = 56 * 1024 * 1024
NEG = -1e30


def _dot(a, b):
    return jnp.dot(a, b, preferred_element_type=F32)


def _dot_t(a, b):
    return lax.dot_general(a, b, (((1,), (1,)), ((), ())), preferred_element_type=F32)


def _split(a):
    hi = a.astype(BF16)
    lo = (a - hi.astype(F32)).astype(BF16)
    return hi, lo


def _dot3(a, b):
    ah, al = _split(a)
    bh, bl = _split(b)
    return _dot(ah, bh) + _dot(ah, bl) + _dot(al, bh)


def _mixer_kernel(prompt, x_ref, g1_ref, win_ref, bgate_ref, lng_ref, lnb_ref, seg_ref, wst_ref, bsp_ref,
                  qg_ref, kg_ref, wpa_ref, *outs):
    if prompt:
        pa_ref, gb_ref, k_ref, qt_ref, kt_ref, vt_ref, km_ref = outs
    else:
        pa_ref, gb_ref, q_ref, k_ref, v_ref, va_ref = outs
    x = x_ref[...]
    ms = jnp.mean(x * x, axis=-1, keepdims=True)
    xb = ((x * lax.rsqrt(ms + EPS)) * g1_ref[...]).astype(BF16)
    seg = seg_ref[...]

    def proj(off, width):
        return _dot(xb, win_ref[:, off:off + width])

    def segmean(a):
        return _dot(a.astype(BF16), seg)

    u = jax.nn.gelu(proj(OFF_U, A_WIDTH))
    va = jax.nn.gelu(proj(OFF_VA, A_WIDTH))
    xc = va - segmean(va)
    va = xc * lax.rsqrt(segmean(xc * xc) + EPS) * lng_ref[...] + lnb_ref[...]
    if not prompt:
        va_ref[...] = va

    wst = wst_ref[...]
    grp = lax.broadcasted_iota(jnp.int32, (CHUNK, A_WIDTH), 1) // A_GROUP_DIM
    ya = []
    for c in range(ROW_TILE // CHUNK):
        rows = slice(c * CHUNK, (c + 1) * CHUNK)
        r = _dot(wst, va[rows].astype(BF16))
        mixed = bsp_ref[...]
        for g in range(A_GROUPS):
            mixed = mixed + jnp.where(grp == g, r[g * CHUNK:(g + 1) * CHUNK], 0.0)
        ya.append(u[rows] * mixed)
    ya = jnp.concatenate(ya, axis=0)

    gates_a = jax.nn.sigmoid(proj(OFF_GATE, D_MODEL) + bgate_ref[:, :D_MODEL])
    pa_ref[...] = gates_a * _dot(ya.astype(BF16), wpa_ref[...])
    gb_ref[...] = jax.nn.sigmoid(proj(OFF_GATE + D_MODEL, D_MODEL) + bgate_ref[:, D_MODEL:])

    zq = proj(OFF_Q, B_WIDTH)
    q = zq * lax.rsqrt(segmean(zq * zq) + EPS) * qg_ref[...]
    zk = proj(OFF_K, B_WIDTH)
    k = zk * lax.rsqrt(segmean(zk * zk) + EPS) * kg_ref[...]
    v = proj(OFF_VB, B_WIDTH)
    k_ref[...] = k
    if prompt:
        qt_ref[...] = q.T
        kt_ref[...] = k.T
        vt_ref[...] = v.T
        km_ref[0] = jnp.mean(k, axis=0, keepdims=True)
    else:
        v_ref[...] = v
        q_ref[...] = q


def _mixer(x2d, prm, wst, bsp, prompt, seq=None):
    n = x2d.shape[0]
    assert n % ROW_TILE == 0
    nt = n // ROW_TILE
    full = lambda a: pl.BlockSpec(a.shape, lambda i: (0,) * a.ndim)
    rows = lambda w: pl.BlockSpec((ROW_TILE, w), lambda i: (i, 0))
    consts = [prm['g1'], prm['win'], prm['bgate'], prm['lng'], prm['lnb'], prm['seg'], wst, bsp,
              prm['qg'], prm['kg'], prm['wpa']]
    out_shape = [jax.ShapeDtypeStruct((n, D_MODEL), F32), jax.ShapeDtypeStruct((n, D_MODEL), F32)]
    out_specs = [rows(D_MODEL), rows(D_MODEL)]
    if prompt:
        tps = seq // ROW_TILE
        out_shape += [jax.ShapeDtypeStruct((n, B_WIDTH), F32)]
        out_specs += [rows(B_WIDTH)]
        out_shape += [jax.ShapeDtypeStruct((n // seq, B_WIDTH, seq), F32)] * 3
        out_specs += [pl.BlockSpec((None, B_WIDTH, ROW_TILE), lambda i: (i // tps, 0, i % tps))] * 3
        out_shape += [jax.ShapeDtypeStruct((nt, 1, B_WIDTH), F32)]
        out_specs += [pl.BlockSpec((1, 1, B_WIDTH), lambda i: (i, 0, 0))]
    else:
        out_shape += [jax.ShapeDtypeStruct((n, B_WIDTH), F32)] * 3
        out_specs += [rows(B_WIDTH)] * 3
        out_shape += [jax.ShapeDtypeStruct((n, A_WIDTH), F32)]
        out_specs += [rows(A_WIDTH)]
    return pl.pallas_call(
        functools.partial(_mixer_kernel, prompt),
        grid=(nt,),
        in_specs=[rows(D_MODEL)] + [full(a) for a in consts],
        out_specs=out_specs,
        out_shape=out_shape,
        compiler_params=pltpu.CompilerParams(dimension_semantics=("arbitrary",), vmem_limit_bytes=VMEM_LIMIT),
        name="mixer_p" if prompt else "mixer_s",
    )(x2d, *consts)


LOG2E = 1.4426950408889634


def _moba_prompt_kernel(n_sel, qt_ref, k_ref, vt_ref, km_ref, o_ref, bias_ref):
    i = pl.program_id(2)
    blk = MOBA_BLOCK
    nb = km_ref.shape[0]
    qt2 = qt_ref[...]
    km2 = km_ref[...]
    row = lax.broadcasted_iota(jnp.int32, (PAIR, blk), 0)
    brow = lax.broadcasted_iota(jnp.int32, (nb, blk), 0)
    hrows = [slice(hh * HEAD_DIM, (hh + 1) * HEAD_DIM) for hh in range(2)]
    qs = []
    for hh in range(2):
        qm = jnp.where((row // HEAD_DIM) == hh, qt2, 0.0)
        gate = jnp.where(brow < i, _dot3(km2, qm), -jnp.inf)
        bias = jnp.full((nb, blk), NEG, F32)
        for r in range(n_sel):
            mx = jnp.max(gate, axis=0, keepdims=True)
            idx = jnp.min(jnp.where(gate == mx, brow, nb), axis=0, keepdims=True)
            hit = brow == idx
            bias = jnp.where(hit & (r < i), 0.0, bias)
            gate = jnp.where(hit, -jnp.inf, gate)
        bias_ref[hh] = bias
        qs.append((qm * (SCALE * LOG2E)).astype(BF16))

    own = pl.multiple_of(i * blk, blk)
    kb = k_ref[pl.ds(own, blk), :].astype(BF16)
    kpos = lax.broadcasted_iota(jnp.int32, (blk, blk), 0)
    qpos = lax.broadcasted_iota(jnp.int32, (blk, blk), 1)
    s_own = [_dot(kb, qs[hh]) for hh in range(2)]
    state = []
    for hh in range(2):
        s = jnp.where(kpos <= qpos, s_own[hh], NEG)
        m = jnp.max(s, axis=0, keepdims=True)
        p = jnp.exp2(s - m)
        l = jnp.sum(p, axis=0, keepdims=True)
        acc = _dot(vt_ref[hrows[hh], pl.ds(own, blk)].astype(BF16), p.astype(BF16))
        state.append((m, l, acc))

    def body(c, carry):
        off = pl.multiple_of(c * (2 * blk), 2 * blk)
        kc = k_ref[pl.ds(off, 2 * blk), :].astype(BF16)
        s2 = [_dot(kc, qs[hh]) for hh in range(2)]
        soft = []
        for hh in range(2):
            m, l, acc = carry[hh]
            s_lo = s2[hh][:blk] + bias_ref[hh, pl.ds(2 * c, 1), :]
            s_hi = s2[hh][blk:] + bias_ref[hh, pl.ds(2 * c + 1, 1), :]
            m_new = jnp.maximum(m, jnp.maximum(jnp.max(s_lo, axis=0, keepdims=True),
                                               jnp.max(s_hi, axis=0, keepdims=True)))
            alpha = jnp.exp2(m - m_new)
            p_lo = jnp.exp2(s_lo - m_new)
            p_hi = jnp.exp2(s_hi - m_new)
            l = alpha * l + jnp.sum(p_lo, axis=0, keepdims=True) + jnp.sum(p_hi, axis=0, keepdims=True)
            soft.append((m_new, l, alpha, jnp.concatenate([p_lo, p_hi], axis=0).astype(BF16)))
        new = []
        for hh in range(2):
            m_new, l, alpha, p = soft[hh]
            acc = alpha * carry[hh][2] + _dot(vt_ref[hrows[hh], pl.ds(off, 2 * blk)].astype(BF16), p)
            new.append((m_new, l, acc))
        return tuple(new)

    state = lax.fori_loop(0, (i + 1) // 2, body, tuple(state))
    o_ref[...] = jnp.concatenate([acc / l for (_, l, acc) in state], axis=0).T


def _moba_prompt(qt, k, vt, kmean, bsz, seq):
    nb = seq // MOBA_BLOCK
    assert nb % 2 == 0
    n_sel = min(MOBA_TOPK, nb - 1)
    npair = B_WIDTH // PAIR
    return pl.pallas_call(
        functools.partial(_moba_prompt_kernel, n_sel),
        grid=(bsz, npair, nb),
        in_specs=[
            pl.BlockSpec((None, PAIR, MOBA_BLOCK), lambda b, p, i: (b, p, i)),
            pl.BlockSpec((seq, PAIR), lambda b, p, i: (b, p)),
            pl.BlockSpec((None, PAIR, seq), lambda b, p, i: (b, p, 0)),
            pl.BlockSpec((nb, PAIR), lambda b, p, i: (b, p)),
        ],
        out_specs=pl.BlockSpec((MOBA_BLOCK, PAIR), lambda b, p, i: (b * nb + i, p)),
        out_shape=jax.ShapeDtypeStruct((bsz * seq, B_WIDTH), F32),
        scratch_shapes=[pltpu.VMEM((2, nb, MOBA_BLOCK), F32)],
        compiler_params=pltpu.CompilerParams(dimension_semantics=("arbitrary",) * 3, vmem_limit_bytes=VMEM_LIMIT),
        name="moba_p",
    )(qt, k, vt, kmean)


SAMPLE_BLOCKS_PER_STEP = 8
PAGES_PER_BLOCK = MOBA_BLOCK // PAGE_SIZE


def _diag_fold(o_all):
    rows = o_all.shape[0]
    n_new = rows // B_HEADS
    rh = lax.broadcasted_iota(jnp.int32, (rows, B_WIDTH), 0) // n_new
    lh = lax.broadcasted_iota(jnp.int32, (rows, B_WIDTH), 1) // HEAD_DIM
    o = jnp.where(rh == lh, o_all, 0.0)
    out = o[:, 0:LANES]
    for t in range(1, B_WIDTH // LANES):
        out = out + o[:, t * LANES:(t + 1) * LANES]
    return out


def _moba_sample_kernel(n_new, nb_past, n_sel, pt_ref, q_ref, kn_ref, vn_ref, *rest):
    npg = SAMPLE_BLOCKS_PER_STEP * PAGES_PER_BLOCK
    k_refs = rest[:npg]
    v_refs = rest[npg:2 * npg]
    o_ref = rest[2 * npg]
    g_scr, m_scr, l_scr, o_scr = rest[2 * npg + 1:]
    step = pl.program_id(1)
    rows = B_HEADS * n_new
    rh = lax.broadcasted_iota(jnp.int32, (rows, B_WIDTH), 0) // n_new
    lh = lax.broadcasted_iota(jnp.int32, (rows, B_WIDTH), 1) // HEAD_DIM
    q = q_ref[...]
    qbd = jnp.where(rh == lh, jnp.concatenate([q] * B_HEADS, axis=0), 0.0).astype(BF16)
    lane = lax.broadcasted_iota(jnp.int32, (rows, LANES), 1)

    @pl.when(step == 0)
    def _():
        g_scr[...] = jnp.zeros_like(g_scr)
        m_scr[...] = jnp.zeros_like(m_scr)
        l_scr[...] = jnp.zeros_like(l_scr)

    s_pages = [_dot(qbd, k_refs[j][...].astype(BF16)) for j in range(npg)]
    probs, gates, maxes, sums = [], [], [], []
    for bi in range(SAMPLE_BLOCKS_PER_STEP):
        blk_pages = s_pages[bi * PAGES_PER_BLOCK:(bi + 1) * PAGES_PER_BLOCK]
        raw = blk_pages[0]
        mx = blk_pages[0]
        for sp in blk_pages[1:]:
            raw = raw + sp
            mx = jnp.maximum(mx, sp)
        gates.append(jnp.sum(raw, axis=1, keepdims=True) * (1.0 / MOBA_BLOCK))
        m_n = jnp.max(mx, axis=1, keepdims=True) * SCALE
        l_n = jnp.zeros((rows, 1), F32)
        for sp in blk_pages:
            p = jnp.exp(sp * SCALE - m_n)
            l_n = l_n + jnp.sum(p, axis=1, keepdims=True)
            probs.append(p.astype(BF16))
        maxes.append(m_n)
        sums.append(l_n)
    outs = []
    for bi in range(SAMPLE_BLOCKS_PER_STEP):
        o_all = _dot_t(probs[bi * PAGES_PER_BLOCK], v_refs[bi * PAGES_PER_BLOCK][...].astype(BF16))
        for pg in range(1, PAGES_PER_BLOCK):
            j = bi * PAGES_PER_BLOCK + pg
            o_all = o_all + _dot_t(probs[j], v_refs[j][...].astype(BF16))
        outs.append(o_all)
    g_new, m_new, l_new = g_scr[0:rows], m_scr[0:rows], l_scr[0:rows]
    for bi in range(SAMPLE_BLOCKS_PER_STEP):
        n = step * SAMPLE_BLOCKS_PER_STEP + bi
        o_scr[n] = _diag_fold(outs[bi])
        g_new = jnp.where(lane == n_new + n, gates[bi], g_new)
        m_new = jnp.where(lane == n_new + n, maxes[bi], m_new)
        l_new = jnp.where(lane == n_new + n, sums[bi], l_new)
    g_scr[0:rows] = g_new
    m_scr[0:rows] = m_new
    l_scr[0:rows] = l_new

    @pl.when(step == pl.num_programs(1) - 1)
    def _():
        ent = lax.broadcasted_iota(jnp.int32, (LANES, LANES), 0)
        col = lax.broadcasted_iota(jnp.int32, (LANES, LANES), 1)
        is_own = ent < n_new
        is_blk = (ent >= n_new) & (ent < n_new + nb_past)
        gate = jnp.where(is_blk, g_scr[...].T, -jnp.inf)
        sel = jnp.zeros((LANES, LANES), jnp.bool_)
        for _ in range(n_sel):
            mx = jnp.max(gate, axis=0, keepdims=True)
            idx = jnp.min(jnp.where(gate == mx, ent, LANES), axis=0, keepdims=True)
            hit = ent == idx
            sel = sel | hit
            gate = jnp.where(hit, -jnp.inf, gate)
        qpad = jnp.concatenate([qbd, jnp.zeros((LANES - rows, B_WIDTH), BF16)], axis=0)
        s_own = _dot_t(kn_ref[...].astype(BF16), qpad) * SCALE
        st = jnp.concatenate([s_own, m_scr[...].T[n_new:]], axis=0)
        valid = (is_own & (ent <= col % n_new)) | (is_blk & sel)
        st = jnp.where(valid, st, NEG)
        m = jnp.max(st, axis=0, keepdims=True)
        w = jnp.where(valid, jnp.exp(st - m), 0.0)
        den = jnp.sum(w * jnp.where(is_own, 1.0, l_scr[...].T), axis=0, keepdims=True)
        wn = (w / den).T[0:rows]
        num = _diag_fold(_dot(wn[:, 0:n_new].astype(BF16), vn_ref[...].astype(BF16)))
        for nblk in range(nb_past):
            num = num + wn[:, n_new + nblk:n_new + nblk + 1] * o_scr[nblk]
        out = jnp.where(rh == lh, jnp.concatenate([num] * (B_WIDTH // LANES), axis=1), 0.0)
        o_ref[...] = jnp.sum(out.reshape(B_HEADS, n_new, B_WIDTH), axis=0)


def _moba_sample(q, k_new, v_new, cache_kt, cache_vt, page_table, past_len):
    db, n_new, _ = q.shape
    assert past_len % MOBA_BLOCK == 0
    nb_past = past_len // MOBA_BLOCK
    assert nb_past % SAMPLE_BLOCKS_PER_STEP == 0 and n_new % 8 == 0
    assert n_new + nb_past <= LANES and B_HEADS * n_new <= LANES
    n_sel = min(MOBA_TOPK, nb_past)
    nsteps = nb_past // SAMPLE_BLOCKS_PER_STEP
    npg = SAMPLE_BLOCKS_PER_STEP * PAGES_PER_BLOCK
    rows = B_HEADS * n_new

    def page_map(j, b, s, pt):
        return (pt[b, s * npg + j], 0, 0)

    page_specs = [pl.BlockSpec((None, B_WIDTH, PAGE_SIZE), functools.partial(page_map, j)) for j in range(npg)]
    tok_spec = pl.BlockSpec((None, n_new, B_WIDTH), lambda b, s, pt: (b, 0, 0))
    grid_spec = pltpu.PrefetchScalarGridSpec(
        num_scalar_prefetch=1,
        grid=(db, nsteps),
        in_specs=[tok_spec] * 3 + page_specs + page_specs,
        out_specs=tok_spec,
        scratch_shapes=[pltpu.VMEM((LANES, LANES), F32)] * 3 + [pltpu.VMEM((nb_past, rows, LANES), F32)],
    )
    return pl.pallas_call(
        functools.partial(_moba_sample_kernel, n_new, nb_past, n_sel),
        grid_spec=grid_spec,
        out_shape=jax.ShapeDtypeStruct((db, n_new, B_WIDTH), F32),
        compiler_params=pltpu.CompilerParams(dimension_semantics=("arbitrary",) * 2, vmem_limit_bytes=VMEM_LIMIT),
        name="moba_s",
    )(page_table, q, k_new, v_new, *([cache_kt] * npg), *([cache_vt] * npg))


def _merge_kernel(x_ref, pa_ref, gb_ref, yb_ref, wpb_ref, wo_ref, g2_ref, wr_ref, br_ref, h_ref, xn_ref, cmb_ref):
    mix = pa_ref[...] + gb_ref[...] * _dot(yb_ref[...].astype(BF16), wpb_ref[...])
    h = x_ref[...] + _dot(mix.astype(BF16), wo_ref[...])
    h_ref[...] = h
    ms = jnp.mean(h * h, axis=-1, keepdims=True)
    xn = (h * lax.rsqrt(ms + EPS)) * g2_ref[...]
    xn_ref[...] = xn.astype(BF16)
    logits = _dot3(xn, wr_ref[...]) + br_ref[...]
    lane = lax.broadcasted_iota(jnp.int32, logits.shape, 1)
    is_grp = (lane >= N_EXPERTS) & (lane < N_EXPERTS + MOE_GROUPS)
    gl = jnp.where(is_grp, logits, -jnp.inf)
    gmax = jnp.max(gl, axis=1, keepdims=True)
    grp = jnp.min(jnp.where(gl == gmax, lane, LANES), axis=1, keepdims=True) - N_EXPERTS
    p_grp = 1.0 / jnp.sum(jnp.exp(gl - gmax), axis=1, keepdims=True)
    el = jnp.where((lane // MOE_PER_GROUP == grp) & (lane < N_EXPERTS), logits, -jnp.inf)
    v1 = jnp.max(el, axis=1, keepdims=True)
    i1 = jnp.min(jnp.where(el == v1, lane, LANES), axis=1, keepdims=True)
    el2 = jnp.where(lane == i1, -jnp.inf, el)
    v2 = jnp.max(el2, axis=1, keepdims=True)
    i2 = jnp.min(jnp.where(el2 == v2, lane, LANES), axis=1, keepdims=True)
    e2 = jnp.exp(v2 - v1)
    w1 = p_grp / (1.0 + e2)
    w2 = p_grp * e2 / (1.0 + e2)
    cmb_ref[...] = jnp.where(lane == i1, w1, 0.0) + jnp.where(lane == i2, w2, 0.0)


def _merge(x2d, pa, gb, yb, prm):
    n = x2d.shape[0]
    nt = n // ROW_TILE
    full = lambda a: pl.BlockSpec(a.shape, lambda i: (0,) * a.ndim)
    rows = lambda w: pl.BlockSpec((ROW_TILE, w), lambda i: (i, 0))
    consts = [prm['wpb'], prm['wo'], prm['g2'], prm['wr'], prm['br']]
    return pl.pallas_call(
        _merge_kernel,
        grid=(nt,),
        in_specs=[rows(D_MODEL), rows(D_MODEL), rows(D_MODEL), rows(B_WIDTH)] + [full(a) for a in consts],
        out_specs=[rows(D_MODEL), rows(D_MODEL), rows(LANES)],
        out_shape=[jax.ShapeDtypeStruct((n, D_MODEL), F32), jax.ShapeDtypeStruct((n, D_MODEL), BF16),
                   jax.ShapeDtypeStruct((n, LANES), F32)],
        compiler_params=pltpu.CompilerParams(dimension_semantics=("arbitrary",), vmem_limit_bytes=VMEM_LIMIT),
        name="merge",
    )(x2d, pa, gb, yb, *consts)


def _moe_kernel(xn_ref, h_ref, cmb_ref, w1_ref, w3_ref, w2_ref, o_ref):
    e = pl.program_id(1)

    @pl.when(e == 0)
    def _():
        o_ref[...] = h_ref[...]

    x = xn_ref[...]
    hid = jax.nn.silu(_dot(x, w1_ref[...])) * _dot(x, w3_ref[...])
    cmb = cmb_ref[...]
    lane = lax.broadcasted_iota(jnp.int32, cmb.shape, 1)
    col = jnp.sum(jnp.where(lane == e, cmb, 0.0), axis=1, keepdims=True)
    o_ref[...] += _dot((hid * col).astype(BF16), w2_ref[...])


def _moe(xn, h, cmb, prm):
    n = xn.shape[0]
    tm = min(MOE_ROW_TILE, n)
    assert n % tm == 0
    rows = lambda w: pl.BlockSpec((tm, w), lambda i, e: (i, 0))
    return pl.pallas_call(
        _moe_kernel,
        grid=(n // tm, N_EXPERTS),
        in_specs=[rows(D_MODEL), rows(D_MODEL), rows(LANES),
                  pl.BlockSpec((None, D_MODEL, D_EXPERT), lambda i, e: (e, 0, 0)),
                  pl.BlockSpec((None, D_MODEL, D_EXPERT), lambda i, e: (e, 0, 0)),
                  pl.BlockSpec((None, D_EXPERT, D_MODEL), lambda i, e: (e, 0, 0))],
        out_specs=rows(D_MODEL),
        out_shape=jax.ShapeDtypeStruct((n, D_MODEL), F32),
        compiler_params=pltpu.CompilerParams(dimension_semantics=("arbitrary",) * 2, vmem_limit_bytes=VMEM_LIMIT),
        name="moe",
    )(xn, h, cmb, prm['w1'], prm['w3'], prm['w2'])


def _prepare(norm1_g, w_in, b_gate, ln_v_g, ln_v_b, qn_g, kn_g, w_pa, w_pb, w_o, norm2_g,
             w_rg, b_rg, w_re, b_re, w1, w3, w2):
    seg = jnp.kron(jnp.eye(B_HEADS, dtype=F32), jnp.full((HEAD_DIM, HEAD_DIM), 1.0 / HEAD_DIM, F32))
    pad = LANES - N_EXPERTS - MOE_GROUPS
    return dict(
        g1=norm1_g.reshape(1, D_MODEL), win=w_in.astype(BF16), bgate=b_gate.reshape(1, 2 * D_MODEL),
        lng=ln_v_g.reshape(1, A_WIDTH), lnb=ln_v_b.reshape(1, A_WIDTH), seg=seg.astype(BF16),
        qg=jnp.tile(qn_g, B_HEADS).reshape(1, B_WIDTH), kg=jnp.tile(kn_g, B_HEADS).reshape(1, B_WIDTH),
        wpa=w_pa.astype(BF16), wpb=w_pb.astype(BF16), wo=w_o.astype(BF16), g2=norm2_g.reshape(1, D_MODEL),
        wr=jnp.pad(jnp.concatenate([w_re, w_rg], axis=1), ((0, 0), (0, pad))),
        br=jnp.pad(jnp.concatenate([b_re, b_rg]), (0, pad)).reshape(1, LANES),
        w1=w1.astype(BF16), w3=w3.astype(BF16), w2=w2.astype(BF16),
    )


def _spatial_weights(w_sp, b_sp, length):
    reps = CHUNK // length
    tril = jnp.tril(jnp.ones((length, length), F32))
    w = w_sp[:, :length, :length] * tril
    w = jax.vmap(lambda m: jnp.kron(jnp.eye(reps, dtype=F32), m))(w)
    bias = jnp.tile(b_sp[:, :length].T, (reps, 1))
    return w.reshape(A_GROUPS * CHUNK, CHUNK).astype(BF16), jnp.repeat(bias, A_GROUP_DIM, axis=1)


def kernel(x_prompt, x_sample, cache_k, cache_v, page_table, norm1_g, w_in, b_gate, ln_v_g, ln_v_b, w_sp, b_sp,
           qn_g, kn_g, w_pa, w_pb, w_o, norm2_g, w_rg, b_rg, w_re, b_re, w1, w3, w2):
    bsz, seq, _ = x_prompt.shape
    db, n_new, _ = x_sample.shape
    past_len = page_table.shape[1] * PAGE_SIZE
    prm = _prepare(norm1_g, w_in, b_gate, ln_v_g, ln_v_b, qn_g, kn_g, w_pa, w_pb, w_o, norm2_g,
                   w_rg, b_rg, w_re, b_re, w1, w3, w2)

    xp = x_prompt.reshape(bsz * seq, D_MODEL)
    wst, bsp = _spatial_weights(w_sp, b_sp, CHUNK)
    pa, gb, k_p, qt, kt, vt, kmean = _mixer(xp, prm, wst, bsp, True, seq)
    yb = _moba_prompt(qt, k_p, vt, kmean.reshape(-1, B_WIDTH), bsz, seq)
    h, xn, cmb = _merge(xp, pa, gb, yb, prm)
    y_p = _moe(xn, h, cmb, prm)

    xs = x_sample.reshape(db * n_new, D_MODEL)
    wst, bsp = _spatial_weights(w_sp, b_sp, n_new)
    pa, gb, q_s, k_s, v_s, va_s = _mixer(xs, prm, wst, bsp, False)
    n_phys = cache_k.shape[0]
    ckt = jnp.transpose(cache_k, (0, 2, 3, 1)).reshape(n_phys, B_WIDTH, PAGE_SIZE)
    cvt = jnp.transpose(cache_v, (0, 2, 3, 1)).reshape(n_phys, B_WIDTH, PAGE_SIZE)
    tok3 = lambda a: a.reshape(db, n_new, B_WIDTH)
    yb = _moba_sample(tok3(q_s), tok3(k_s), tok3(v_s), ckt, cvt, page_table, past_len)
    h, xn, cmb = _merge(xs, pa, gb, yb.reshape(db * n_new, B_WIDTH), prm)
    y_s = _moe(xn, h, cmb, prm)

    heads = lambda a, lead: a.reshape(*lead, B_HEADS, HEAD_DIM)
    from_t = lambda a: jnp.transpose(a.reshape(bsz, B_HEADS, HEAD_DIM, seq), (0, 3, 1, 2))
    return (y_p.reshape(bsz, seq, D_MODEL), y_s.reshape(db, n_new, D_MODEL),
            from_t(kt), from_t(vt),
            heads(k_s, (db, n_new)), heads(v_s, (db, n_new)),
            va_s.reshape(db, n_new, A_WIDTH))
```

```python
import functools

import jax
import jax.numpy as jnp
from jax import lax
from jax.experimental import pallas as pl
from jax.experimental.pallas import tpu as pltpu

F32 = jnp.float32
BF16 = jnp.bfloat16

D_MODEL = 1024
A_GROUPS = 8
A_GROUP_DIM = 64
A_WIDTH = A_GROUPS * A_GROUP_DIM
CHUNK = 128
B_HEADS = 8
HEAD_DIM = 64
B_WIDTH = B_HEADS * HEAD_DIM
MOBA_BLOCK = 256
MOBA_TOPK = 3
PAGE_SIZE = 128
SCALE = HEAD_DIM ** -0.5
MOE_GROUPS = 4
MOE_PER_GROUP = 4
N_EXPERTS = MOE_GROUPS * MOE_PER_GROUP
D_EXPERT = 512
EPS = 1e-6
OFF_U = 0
OFF_VA = OFF_U + A_WIDTH
OFF_Q = OFF_VA + A_WIDTH
OFF_K = OFF_Q + B_WIDTH
OFF_VB = OFF_K + B_WIDTH
OFF_GATE = OFF_VB + B_WIDTH

LANES = 128
ROW_TILE = 256
PAIR = 2 * HEAD_DIM
VMEM_LIMIT = 56 * 1024 * 1024
NEG = -1e30


def _dot(a, b):
    return jnp.dot(a, b, preferred_element_type=F32)


def _dot_t(a, b):
    return lax.dot_general(a, b, (((1,), (1,)), ((), ())), preferred_element_type=F32)


def _split(a):
    hi = a.astype(BF16)
    lo = (a - hi.astype(F32)).astype(BF16)
    return hi, lo


def _dot3(a, b):
    ah, al = _split(a)
    bh, bl = _split(b)
    return _dot(ah, bh) + _dot(ah, bl) + _dot(al, bh)


def _mixer_kernel(prompt, x_ref, g1_ref, win_ref, bgate_ref, lng_ref, lnb_ref, seg_ref, wst_ref, bsp_ref,
                  qg_ref, kg_ref, wpa_ref, *outs):
    if prompt:
        pa_ref, gb_ref, k_ref, qt_ref, kt_ref, vt_ref, km_ref = outs
    else:
        pa_ref, gb_ref, q_ref, k_ref, v_ref, va_ref = outs
    x = x_ref[...]
    ms = jnp.mean(x * x, axis=-1, keepdims=True)
    xb = ((x * lax.rsqrt(ms + EPS)) * g1_ref[...]).astype(BF16)
    seg = seg_ref[...]

    def proj(off, width):
        return _dot(xb, win_ref[:, off:off + width])

    def segmean(a):
        return _dot(a.astype(BF16), seg)

    u = jax.nn.gelu(proj(OFF_U, A_WIDTH))
    va = jax.nn.gelu(proj(OFF_VA, A_WIDTH))
    xc = va - segmean(va)
    va = xc * lax.rsqrt(segmean(xc * xc) + EPS) * lng_ref[...] + lnb_ref[...]
    if not prompt:
        va_ref[...] = va

    wst = wst_ref[...]
    grp = lax.broadcasted_iota(jnp.int32, (CHUNK, A_WIDTH), 1) // A_GROUP_DIM
    ya = []
    for c in range(ROW_TILE // CHUNK):
        rows = slice(c * CHUNK, (c + 1) * CHUNK)
        r = _dot(wst, va[rows].astype(BF16))
        mixed = bsp_ref[...]
        for g in range(A_GROUPS):
            mixed = mixed + jnp.where(grp == g, r[g * CHUNK:(g + 1) * CHUNK], 0.0)
        ya.append(u[rows] * mixed)
    ya = jnp.concatenate(ya, axis=0)

    gates_a = jax.nn.sigmoid(proj(OFF_GATE, D_MODEL) + bgate_ref[:, :D_MODEL])
    pa_ref[...] = gates_a * _dot(ya.astype(BF16), wpa_ref[...])
    gb_ref[...] = jax.nn.sigmoid(proj(OFF_GATE + D_MODEL, D_MODEL) + bgate_ref[:, D_MODEL:])

    zq = proj(OFF_Q, B_WIDTH)
    q = zq * lax.rsqrt(segmean(zq * zq) + EPS) * qg_ref[...]
    zk = proj(OFF_K, B_WIDTH)
    k = zk * lax.rsqrt(segmean(zk * zk) + EPS) * kg_ref[...]
    v = proj(OFF_VB, B_WIDTH)
    k_ref[...] = k
    if prompt:
        qt_ref[...] = q.T
        kt_ref[...] = k.T
        vt_ref[...] = v.T
        km_ref[0] = jnp.mean(k, axis=0, keepdims=True)
    else:
        v_ref[...] = v
        q_ref[...] = q


def _mixer(x2d, prm, wst, bsp, prompt, seq=None):
    n = x2d.shape[0]
    assert n % ROW_TILE == 0
    nt = n // ROW_TILE
    full = lambda a: pl.BlockSpec(a.shape, lambda i: (0,) * a.ndim)
    rows = lambda w: pl.BlockSpec((ROW_TILE, w), lambda i: (i, 0))
    consts = [prm['g1'], prm['win'], prm['bgate'], prm['lng'], prm['lnb'], prm['seg'], wst, bsp,
              prm['qg'], prm['kg'], prm['wpa']]
    out_shape = [jax.ShapeDtypeStruct((n, D_MODEL), F32), jax.ShapeDtypeStruct((n, D_MODEL), F32)]
    out_specs = [rows(D_MODEL), rows(D_MODEL)]
    if prompt:
        tps = seq // ROW_TILE
        out_shape += [jax.ShapeDtypeStruct((n, B_WIDTH), F32)]
        out_specs += [rows(B_WIDTH)]
        out_shape += [jax.ShapeDtypeStruct((n // seq, B_WIDTH, seq), F32)] * 3
        out_specs += [pl.BlockSpec((None, B_WIDTH, ROW_TILE), lambda i: (i // tps, 0, i % tps))] * 3
        out_shape += [jax.ShapeDtypeStruct((nt, 1, B_WIDTH), F32)]
        out_specs += [pl.BlockSpec((1, 1, B_WIDTH), lambda i: (i, 0, 0))]
    else:
        out_shape += [jax.ShapeDtypeStruct((n, B_WIDTH), F32)] * 3
        out_specs += [rows(B_WIDTH)] * 3
        out_shape += [jax.ShapeDtypeStruct((n, A_WIDTH), F32)]
        out_specs += [rows(A_WIDTH)]
    return pl.pallas_call(
        functools.partial(_mixer_kernel, prompt),
        grid=(nt,),
        in_specs=[rows(D_MODEL)] + [full(a) for a in consts],
        out_specs=out_specs,
        out_shape=out_shape,
        compiler_params=pltpu.CompilerParams(dimension_semantics=("arbitrary",), vmem_limit_bytes=VMEM_LIMIT),
        name="mixer_p" if prompt else "mixer_s",
    )(x2d, *consts)


LOG2E = 1.4426950408889634


def _moba_prompt_kernel(n_sel, qt_ref, k_ref, vt_ref, km_ref, o_ref, bias_ref):
    i = pl.program_id(2)
    blk = MOBA_BLOCK
    nb = km_ref.shape[0]
    qt2 = qt_ref[...]
    km2 = km_ref[...]
    row = lax.broadcasted_iota(jnp.int32, (PAIR, blk), 0)
    brow = lax.broadcasted_iota(jnp.int32, (nb, blk), 0)
    hrows = [slice(hh * HEAD_DIM, (hh + 1) * HEAD_DIM) for hh in range(2)]
    qs = []
    for hh in range(2):
        qm = jnp.where((row // HEAD_DIM) == hh, qt2, 0.0)
        gate = jnp.where(brow < i, _dot3(km2, qm), -jnp.inf)
        bias = jnp.full((nb, blk), NEG, F32)
        for r in range(n_sel):
            mx = jnp.max(gate, axis=0, keepdims=True)
            idx = jnp.min(jnp.where(gate == mx, brow, nb), axis=0, keepdims=True)
            hit = brow == idx
            bias = jnp.where(hit & (r < i), 0.0, bias)
            gate = jnp.where(hit, -jnp.inf, gate)
        bias_ref[hh] = bias
        qs.append((qm * (SCALE * LOG2E)).astype(BF16))

    own = pl.multiple_of(i * blk, blk)
    kb = k_ref[pl.ds(own, blk), :].astype(BF16)
    kpos = lax.broadcasted_iota(jnp.int32, (blk, blk), 0)
    qpos = lax.broadcasted_iota(jnp.int32, (blk, blk), 1)
    s_own = [_dot(kb, qs[hh]) for hh in range(2)]
    state = []
    for hh in range(2):
        s = jnp.where(kpos <= qpos, s_own[hh], NEG)
        m = jnp.max(s, axis=0, keepdims=True)
        p = jnp.exp2(s - m)
        l = jnp.sum(p, axis=0, keepdims=True)
        acc = _dot(vt_ref[hrows[hh], pl.ds(own, blk)].astype(BF16), p.astype(BF16))
        state.append((m, l, acc))

    def body(c, carry):
        off = pl.multiple_of(c * (2 * blk), 2 * blk)
        kc = k_ref[pl.ds(off, 2 * blk), :].astype(BF16)
        s2 = [_dot(kc, qs[hh]) for hh in range(2)]
        soft = []
        for hh in range(2):
            m, l, acc = carry[hh]
            s_lo = s2[hh][:blk] + bias_ref[hh, pl.ds(2 * c, 1), :]
            s_hi = s2[hh][blk:] + bias_ref[hh, pl.ds(2 * c + 1, 1), :]
            m_new = jnp.maximum(m, jnp.maximum(jnp.max(s_lo, axis=0, keepdims=True),
                                               jnp.max(s_hi, axis=0, keepdims=True)))
            alpha = jnp.exp2(m - m_new)
            p_lo = jnp.exp2(s_lo - m_new)
            p_hi = jnp.exp2(s_hi - m_new)
            l = alpha * l + jnp.sum(p_lo, axis=0, keepdims=True) + jnp.sum(p_hi, axis=0, keepdims=True)
            soft.append((m_new, l, alpha, jnp.concatenate([p_lo, p_hi], axis=0).astype(BF16)))
        new = []
        for hh in range(2):
            m_new, l, alpha, p = soft[hh]
            acc = alpha * carry[hh][2] + _dot(vt_ref[hrows[hh], pl.ds(off, 2 * blk)].astype(BF16), p)
            new.append((m_new, l, acc))
        return tuple(new)

    state = lax.fori_loop(0, (i + 1) // 2, body, tuple(state))
    o_ref[...] = jnp.concatenate([acc / l for (_, l, acc) in state], axis=0).T


def _moba_prompt(qt, k, vt, kmean, bsz, seq):
    nb = seq // MOBA_BLOCK
    assert nb % 2 == 0
    n_sel = min(MOBA_TOPK, nb - 1)
    npair = B_WIDTH // PAIR
    return pl.pallas_call(
        functools.partial(_moba_prompt_kernel, n_sel),
        grid=(bsz, npair, nb),
        in_specs=[
            pl.BlockSpec((None, PAIR, MOBA_BLOCK), lambda b, p, i: (b, p, i)),
            pl.BlockSpec((seq, PAIR), lambda b, p, i: (b, p)),
            pl.BlockSpec((None, PAIR, seq), lambda b, p, i: (b, p, 0)),
            pl.BlockSpec((nb, PAIR), lambda b, p, i: (b, p)),
        ],
        out_specs=pl.BlockSpec((MOBA_BLOCK, PAIR), lambda b, p, i: (b * nb + i, p)),
        out_shape=jax.ShapeDtypeStruct((bsz * seq, B_WIDTH), F32),
        scratch_shapes=[pltpu.VMEM((2, nb, MOBA_BLOCK), F32)],
        compiler_params=pltpu.CompilerParams(dimension_semantics=("arbitrary",) * 3, vmem_limit_bytes=VMEM_LIMIT),
        name="moba_p",
    )(qt, k, vt, kmean)


SAMPLE_BLOCKS_PER_STEP = 8
PAGES_PER_BLOCK = MOBA_BLOCK // PAGE_SIZE


def _diag_fold(o_all):
    rows = o_all.shape[0]
    n_new = rows // B_HEADS
    rh = lax.broadcasted_iota(jnp.int32, (rows, B_WIDTH), 0) // n_new
    lh = lax.broadcasted_iota(jnp.int32, (rows, B_WIDTH), 1) // HEAD_DIM
    o = jnp.where(rh == lh, o_all, 0.0)
    out = o[:, 0:LANES]
    for t in range(1, B_WIDTH // LANES):
        out = out + o[:, t * LANES:(t + 1) * LANES]
    return out


def _moba_sample_kernel(n_new, nb_past, n_sel, pt_ref, q_ref, kn_ref, vn_ref, *rest):
    npg = SAMPLE_BLOCKS_PER_STEP * PAGES_PER_BLOCK
    k_refs = rest[:npg]
    v_refs = rest[npg:2 * npg]
    o_ref = rest[2 * npg]
    g_scr, m_scr, l_scr, o_scr = rest[2 * npg + 1:]
    step = pl.program_id(1)
    rows = B_HEADS * n_new
    rh = lax.broadcasted_iota(jnp.int32, (rows, B_WIDTH), 0) // n_new
    lh = lax.broadcasted_iota(jnp.int32, (rows, B_WIDTH), 1) // HEAD_DIM
    q = q_ref[...]
    qbd = jnp.where(rh == lh, jnp.concatenate([q] * B_HEADS, axis=0), 0.0).astype(BF16)
    lane = lax.broadcasted_iota(jnp.int32, (rows, LANES), 1)

    @pl.when(step == 0)
    def _():
        g_scr[...] = jnp.zeros_like(g_scr)
        m_scr[...] = jnp.zeros_like(m_scr)
        l_scr[...] = jnp.zeros_like(l_scr)

    s_pages = [_dot(qbd, k_refs[j][...].astype(BF16)) for j in range(npg)]
    probs, gates, maxes, sums = [], [], [], []
    for bi in range(SAMPLE_BLOCKS_PER_STEP):
        blk_pages = s_pages[bi * PAGES_PER_BLOCK:(bi + 1) * PAGES_PER_BLOCK]
        raw = blk_pages[0]
        mx = blk_pages[0]
        for sp in blk_pages[1:]:
            raw = raw + sp
            mx = jnp.maximum(mx, sp)
        gates.append(jnp.sum(raw, axis=1, keepdims=True) * (1.0 / MOBA_BLOCK))
        m_n = jnp.max(mx, axis=1, keepdims=True) * SCALE
        l_n = jnp.zeros((rows, 1), F32)
        for sp in blk_pages:
            p = jnp.exp(sp * SCALE - m_n)
            l_n = l_n + jnp.sum(p, axis=1, keepdims=True)
            probs.append(p.astype(BF16))
        maxes.append(m_n)
        sums.append(l_n)
    outs = []
    for bi in range(SAMPLE_BLOCKS_PER_STEP):
        o_all = _dot_t(probs[bi * PAGES_PER_BLOCK], v_refs[bi * PAGES_PER_BLOCK][...].astype(BF16))
        for pg in range(1, PAGES_PER_BLOCK):
            j = bi * PAGES_PER_BLOCK + pg
            o_all = o_all + _dot_t(probs[j], v_refs[j][...].astype(BF16))
        outs.append(o_all)
    g_new, m_new, l_new = g_scr[0:rows], m_scr[0:rows], l_scr[0:rows]
    for bi in range(SAMPLE_BLOCKS_PER_STEP):
        n = step * SAMPLE_BLOCKS_PER_STEP + bi
        o_scr[n] = _diag_fold(outs[bi])
        g_new = jnp.where(lane == n_new + n, gates[bi], g_new)
        m_new = jnp.where(lane == n_new + n, maxes[bi], m_new)
        l_new = jnp.where(lane == n_new + n, sums[bi], l_new)
    g_scr[0:rows] = g_new
    m_scr[0:rows] = m_new
    l_scr[0:rows] = l_new

    @pl.when(step == pl.num_programs(1) - 1)
    def _():
        ent = lax.broadcasted_iota(jnp.int32, (LANES, LANES), 0)
        col = lax.broadcasted_iota(jnp.int32, (LANES, LANES), 1)
        is_own = ent < n_new
        is_blk = (ent >= n_new) & (ent < n_new + nb_past)
        gate = jnp.where(is_blk, g_scr[...].T, -jnp.inf)
        sel = jnp.zeros((LANES, LANES), jnp.bool_)
        for _ in range(n_sel):
            mx = jnp.max(gate, axis=0, keepdims=True)
            idx = jnp.min(jnp.where(gate == mx, ent, LANES), axis=0, keepdims=True)
            hit = ent == idx
            sel = sel | hit
            gate = jnp.where(hit, -jnp.inf, gate)
        qpad = jnp.concatenate([qbd, jnp.zeros((LANES - rows, B_WIDTH), BF16)], axis=0)
        s_own = _dot_t(kn_ref[...].astype(BF16), qpad) * SCALE
        st = jnp.concatenate([s_own, m_scr[...].T[n_new:]], axis=0)
        valid = (is_own & (ent <= col % n_new)) | (is_blk & sel)
        st = jnp.where(valid, st, NEG)
        m = jnp.max(st, axis=0, keepdims=True)
        w = jnp.where(valid, jnp.exp(st - m), 0.0)
        den = jnp.sum(w * jnp.where(is_own, 1.0, l_scr[...].T), axis=0, keepdims=True)
        wn = (w / den).T[0:rows]
        num = _diag_fold(_dot(wn[:, 0:n_new].astype(BF16), vn_ref[...].astype(BF16)))
        for nblk in range(nb_past):
            num = num + wn[:, n_new + nblk:n_new + nblk + 1] * o_scr[nblk]
        out = jnp.where(rh == lh, jnp.concatenate([num] * (B_WIDTH // LANES), axis=1), 0.0)
        o_ref[...] = jnp.sum(out.reshape(B_HEADS, n_new, B_WIDTH), axis=0)


def _moba_sample(q, k_new, v_new, cache_kt, cache_vt, page_table, past_len):
    db, n_new, _ = q.shape
    assert past_len % MOBA_BLOCK == 0
    nb_past = past_len // MOBA_BLOCK
    assert nb_past % SAMPLE_BLOCKS_PER_STEP == 0 and n_new % 8 == 0
    assert n_new + nb_past <= LANES and B_HEADS * n_new <= LANES
    n_sel = min(MOBA_TOPK, nb_past)
    nsteps = nb_past // SAMPLE_BLOCKS_PER_STEP
    npg = SAMPLE_BLOCKS_PER_STEP * PAGES_PER_BLOCK
    rows = B_HEADS * n_new

    def page_map(j, b, s, pt):
        return (pt[b, s * npg + j], 0, 0)

    page_specs = [pl.BlockSpec((None, B_WIDTH, PAGE_SIZE), functools.partial(page_map, j)) for j in range(npg)]
    tok_spec = pl.BlockSpec((None, n_new, B_WIDTH), lambda b, s, pt: (b, 0, 0))
    grid_spec = pltpu.PrefetchScalarGridSpec(
        num_scalar_prefetch=1,
        grid=(db, nsteps),
        in_specs=[tok_spec] * 3 + page_specs + page_specs,
        out_specs=tok_spec,
        scratch_shapes=[pltpu.VMEM((LANES, LANES), F32)] * 3 + [pltpu.VMEM((nb_past, rows, LANES), F32)],
    )
    return pl.pallas_call(
        functools.partial(_moba_sample_kernel, n_new, nb_past, n_sel),
        grid_spec=grid_spec,
        out_shape=jax.ShapeDtypeStruct((db, n_new, B_WIDTH), F32),
        compiler_params=pltpu.CompilerParams(dimension_semantics=("arbitrary",) * 2, vmem_limit_bytes=VMEM_LIMIT),
        name="moba_s",
    )(page_table, q, k_new, v_new, *([cache_kt] * npg), *([cache_vt] * npg))


def _merge_kernel(x_ref, pa_ref, gb_ref, yb_ref, wpb_ref, wo_ref, g2_ref, wr_ref, br_ref, h_ref, xn_ref, route_ref):
    mix = pa_ref[...] + gb_ref[...] * _dot(yb_ref[...].astype(BF16), wpb_ref[...])
    h = x_ref[...] + _dot(mix.astype(BF16), wo_ref[...])
    h_ref[...] = h
    ms = jnp.mean(h * h, axis=-1, keepdims=True)
    xn = (h * lax.rsqrt(ms + EPS)) * g2_ref[...]
    xn_ref[...] = xn
    logits = _dot3(xn, wr_ref[...]) + br_ref[...]
    lane = lax.broadcasted_iota(jnp.int32, logits.shape, 1)
    is_grp = (lane >= N_EXPERTS) & (lane < N_EXPERTS + MOE_GROUPS)
    gl = jnp.where(is_grp, logits, -jnp.inf)
    gmax = jnp.max(gl, axis=1, keepdims=True)
    grp = jnp.min(jnp.where(gl == gmax, lane, LANES), axis=1, keepdims=True) - N_EXPERTS
    p_grp = 1.0 / jnp.sum(jnp.exp(gl - gmax), axis=1, keepdims=True)
    el = jnp.where((lane // MOE_PER_GROUP == grp) & (lane < N_EXPERTS), logits, -jnp.inf)
    v1 = jnp.max(el, axis=1, keepdims=True)
    i1 = jnp.min(jnp.where(el == v1, lane, LANES), axis=1, keepdims=True)
    el2 = jnp.where(lane == i1, -jnp.inf, el)
    v2 = jnp.max(el2, axis=1, keepdims=True)
    i2 = jnp.min(jnp.where(el2 == v2, lane, LANES), axis=1, keepdims=True)
    e2 = jnp.exp(v2 - v1)
    w1 = p_grp / (1.0 + e2)
    w2 = p_grp * e2 / (1.0 + e2)
    route = jnp.where(lane == 0, i1.astype(F32), jnp.where(lane == 1, i2.astype(F32), 0.0))
    route_ref[...] = jnp.where(lane == 2, w1, jnp.where(lane == 3, w2, route))


def _merge(x2d, pa, gb, yb, prm):
    n = x2d.shape[0]
    nt = n // ROW_TILE
    full = lambda a: pl.BlockSpec(a.shape, lambda i: (0,) * a.ndim)
    rows = lambda w: pl.BlockSpec((ROW_TILE, w), lambda i: (i, 0))
    consts = [prm['wpb'], prm['wo'], prm['g2'], prm['wr'], prm['br']]
    return pl.pallas_call(
        _merge_kernel,
        grid=(nt,),
        in_specs=[rows(D_MODEL), rows(D_MODEL), rows(D_MODEL), rows(B_WIDTH)] + [full(a) for a in consts],
        out_specs=[rows(D_MODEL), rows(D_MODEL), rows(LANES)],
        out_shape=[jax.ShapeDtypeStruct((n, D_MODEL), F32), jax.ShapeDtypeStruct((n, D_MODEL), F32),
                   jax.ShapeDtypeStruct((n, LANES), F32)],
        compiler_params=pltpu.CompilerParams(dimension_semantics=("arbitrary",), vmem_limit_bytes=VMEM_LIMIT),
        name="merge",
    )(x2d, pa, gb, yb, *consts)


EXPERT_TILE = 256
POS_ROWS = 8
DMA_UNROLL = 8


def _rank_kernel(route_ref, pos_ref, cnt_ref, carry_ref):
    @pl.when(pl.program_id(0) == 0)
    def _():
        carry_ref[...] = jnp.zeros_like(carry_ref)

    route = route_ref[...]
    tm = route.shape[0]
    lane = lax.broadcasted_iota(jnp.int32, (tm, LANES), 1)
    lanef = lane.astype(F32)
    i1, i2 = route[:, 0:1], route[:, 1:2]
    hit1, hit2 = lanef == i1, lanef == i2
    onehot = jnp.where(hit1 | hit2, 1.0, 0.0)
    earlier = lax.broadcasted_iota(jnp.int32, (tm, tm), 1) < lax.broadcasted_iota(jnp.int32, (tm, tm), 0)
    before = _dot(jnp.where(earlier, 1.0, 0.0).astype(BF16), onehot.astype(BF16)) + carry_ref[...]
    r1 = jnp.sum(jnp.where(hit1, before, 0.0), axis=1, keepdims=True)
    r2 = jnp.sum(jnp.where(hit2, before, 0.0), axis=1, keepdims=True)
    carry_ref[...] = carry_ref[...] + jnp.sum(onehot, axis=0, keepdims=True)
    packed = jnp.where(lane == 0, i1, jnp.where(lane == 1, r1, jnp.where(lane == 2, i2, jnp.where(lane == 3, r2, 0.0))))
    pos_ref[...] = packed.T[0:POS_ROWS].astype(jnp.int32)
    cnt_ref[...] = carry_ref[...].astype(jnp.int32)


def _dispatch_kernel(off_ref, cnt_ref, pos_ref, xn_hbm, xs_hbm, zero_ref, zsem, sem):
    i = pl.program_id(0)
    tm = pos_ref.shape[1]

    last = N_EXPERTS - 1
    n_act = off_ref[last] // EXPERT_TILE + (cnt_ref[last] + EXPERT_TILE - 1) // EXPERT_TILE
    kmax = xs_hbm.shape[0] // EXPERT_TILE
    fills = [(cnt_ref[e] % EXPERT_TILE != 0, off_ref[e] + (cnt_ref[e] // EXPERT_TILE) * EXPERT_TILE)
             for e in range(N_EXPERTS)]
    fills += [(n_act + j < kmax, (n_act + j) * EXPERT_TILE) for j in range(N_EXPERTS)]

    def zero_copy(first):
        return pltpu.make_async_copy(zero_ref, xs_hbm.at[pl.ds(pl.multiple_of(first, EXPERT_TILE), EXPERT_TILE)], zsem)

    @pl.when(i == 0)
    def _():
        zero_ref[...] = jnp.zeros_like(zero_ref)
        for needed, first in fills:
            @pl.when(needed)
            def _():
                zero_copy(first).start()
        for needed, first in fills:
            @pl.when(needed)
            def _():
                zero_copy(first).wait()

    def row_copies(r):
        src = xn_hbm.at[pl.ds(i * tm + r, 1)]
        return [pltpu.make_async_copy(src, xs_hbm.at[pl.ds(pos_ref[s, r], 1)], sem)
                for s in range(2)]

    def issue(r, c):
        for cp in row_copies(r):
            cp.start()
        return c

    def drain(r, c):
        for cp in row_copies(r):
            cp.wait()
        return c

    lax.fori_loop(0, tm, issue, 0, unroll=DMA_UNROLL)
    lax.fori_loop(0, tm, drain, 0, unroll=DMA_UNROLL)


def _ffn_kernel(te_ref, nact_ref, xs_ref, w1_ref, w3_ref, w2_ref, ys_ref):
    active = pl.program_id(0) < nact_ref[0]

    @pl.when(active)
    def _():
        x = xs_ref[...].astype(BF16)
        hid = jax.nn.silu(_dot(x, w1_ref[...].astype(BF16))) * _dot(x, w3_ref[...].astype(BF16))
        ys_ref[...] = _dot(hid.astype(BF16), w2_ref[...].astype(BF16))

    @pl.when(jnp.logical_not(active))
    def _():
        ys_ref[...] = jnp.zeros_like(ys_ref)


def _combine_kernel(pos_ref, h_ref, route_ref, ys_hbm, o_ref, g1_ref, g2_ref, sem):
    tm = pos_ref.shape[1]
    bufs = (g1_ref, g2_ref)

    def row_copies(r):
        return [pltpu.make_async_copy(ys_hbm.at[pl.ds(pos_ref[s, r], 1)],
                                      bufs[s].at[pl.ds(r, 1)], sem) for s in range(2)]

    def issue(r, c):
        for cp in row_copies(r):
            cp.start()
        return c

    def drain(r, c):
        for cp in row_copies(r):
            cp.wait()
        return c

    lax.fori_loop(0, tm, issue, 0, unroll=DMA_UNROLL)
    lax.fori_loop(0, tm, drain, 0, unroll=DMA_UNROLL)
    route = route_ref[...]
    o_ref[...] = h_ref[...] + route[:, 2:3] * g1_ref[...] + route[:, 3:4] * g2_ref[...]


def _moe(xn, h, route, w1, w3, w2):
    n = xn.shape[0]
    tm = ROW_TILE
    nt = n // tm
    kmax = 2 * n // EXPERT_TILE + N_EXPERTS
    cparams = pltpu.CompilerParams(dimension_semantics=("arbitrary",), vmem_limit_bytes=VMEM_LIMIT)

    pos, cnt = pl.pallas_call(
        _rank_kernel,
        grid=(nt,),
        in_specs=[pl.BlockSpec((tm, LANES), lambda i: (i, 0))],
        out_specs=[pl.BlockSpec((None, POS_ROWS, tm), lambda i: (i, 0, 0)), pl.BlockSpec((1, LANES), lambda i: (0, 0))],
        out_shape=[jax.ShapeDtypeStruct((nt, POS_ROWS, tm), jnp.int32), jax.ShapeDtypeStruct((1, LANES), jnp.int32)],
        scratch_shapes=[pltpu.VMEM((1, LANES), F32)],
        compiler_params=cparams, name="moe_rank",
    )(route)

    cnt = cnt[0, :N_EXPERTS]
    ntile = (cnt + EXPERT_TILE - 1) // EXPERT_TILE
    cum = jnp.cumsum(ntile)
    off = ((cum - ntile) * EXPERT_TILE).astype(jnp.int32)
    nact = cum[-1:].astype(jnp.int32)
    tile = jnp.minimum(jnp.arange(kmax, dtype=jnp.int32), nact[0] - 1)
    tile_e = jnp.searchsorted(cum, tile, side='right').astype(jnp.int32)
    dest = jnp.concatenate([off[pos[:, 0:1]] + pos[:, 1:2], off[pos[:, 2:3]] + pos[:, 3:4], pos[:, 2:]], axis=1)

    pos_spec = pl.BlockSpec((None, POS_ROWS, tm), lambda i, *_: (i, 0, 0), memory_space=pltpu.SMEM)
    any_spec = pl.BlockSpec(memory_space=pl.ANY)
    xs = pl.pallas_call(
        _dispatch_kernel,
        grid_spec=pltpu.PrefetchScalarGridSpec(
            num_scalar_prefetch=2, grid=(nt,), in_specs=[pos_spec, any_spec], out_specs=any_spec,
            scratch_shapes=[pltpu.VMEM((EXPERT_TILE, D_MODEL), F32), pltpu.SemaphoreType.DMA, pltpu.SemaphoreType.DMA]),
        out_shape=jax.ShapeDtypeStruct((kmax * EXPERT_TILE, D_MODEL), F32),
        compiler_params=cparams, name="moe_dispatch",
    )(off, cnt, dest, xn)

    act = lambda k, te, na: (jnp.minimum(k, na[0] - 1), 0)
    ys = pl.pallas_call(
        _ffn_kernel,
        grid_spec=pltpu.PrefetchScalarGridSpec(
            num_scalar_prefetch=2, grid=(kmax,),
            in_specs=[pl.BlockSpec((EXPERT_TILE, D_MODEL), act),
                      pl.BlockSpec((None, D_MODEL, D_EXPERT), lambda k, te, na: (te[k], 0, 0)),
                      pl.BlockSpec((None, D_MODEL, D_EXPERT), lambda k, te, na: (te[k], 0, 0)),
                      pl.BlockSpec((None, D_EXPERT, D_MODEL), lambda k, te, na: (te[k], 0, 0))],
            out_specs=pl.BlockSpec((EXPERT_TILE, D_MODEL), lambda k, te, na: (k, 0))),
        out_shape=jax.ShapeDtypeStruct((kmax * EXPERT_TILE, D_MODEL), F32),
        compiler_params=cparams, name="moe_ffn",
    )(tile_e, nact, xs, w1, w3, w2)

    rows = lambda w: pl.BlockSpec((tm, w), lambda i, *_: (i, 0))
    return pl.pallas_call(
        _combine_kernel,
        grid_spec=pltpu.PrefetchScalarGridSpec(
            num_scalar_prefetch=0, grid=(nt,),
            in_specs=[pos_spec, rows(D_MODEL), rows(LANES), any_spec],
            out_specs=rows(D_MODEL),
            scratch_shapes=[pltpu.VMEM((tm, D_MODEL), F32), pltpu.VMEM((tm, D_MODEL), F32), pltpu.SemaphoreType.DMA]),
        out_shape=jax.ShapeDtypeStruct((n, D_MODEL), F32),
        compiler_params=cparams, name="moe_combine",
    )(dest, h, route, ys)


def _prepare(norm1_g, w_in, b_gate, ln_v_g, ln_v_b, qn_g, kn_g, w_pa, w_pb, w_o, norm2_g,
             w_rg, b_rg, w_re, b_re):
    seg = jnp.kron(jnp.eye(B_HEADS, dtype=F32), jnp.full((HEAD_DIM, HEAD_DIM), 1.0 / HEAD_DIM, F32))
    pad = LANES - N_EXPERTS - MOE_GROUPS
    return dict(
        g1=norm1_g.reshape(1, D_MODEL), win=w_in.astype(BF16), bgate=b_gate.reshape(1, 2 * D_MODEL),
        lng=ln_v_g.reshape(1, A_WIDTH), lnb=ln_v_b.reshape(1, A_WIDTH), seg=seg.astype(BF16),
        qg=jnp.tile(qn_g, B_HEADS).reshape(1, B_WIDTH), kg=jnp.tile(kn_g, B_HEADS).reshape(1, B_WIDTH),
        wpa=w_pa.astype(BF16), wpb=w_pb.astype(BF16), wo=w_o.astype(BF16), g2=norm2_g.reshape(1, D_MODEL),
        wr=jnp.pad(jnp.concatenate([w_re, w_rg], axis=1), ((0, 0), (0, pad))),
        br=jnp.pad(jnp.concatenate([b_re, b_rg]), (0, pad)).reshape(1, LANES),
    )


def _spatial_weights(w_sp, b_sp, length):
    reps = CHUNK // length
    tril = jnp.tril(jnp.ones((length, length), F32))
    w = w_sp[:, :length, :length] * tril
    w = jax.vmap(lambda m: jnp.kron(jnp.eye(reps, dtype=F32), m))(w)
    bias = jnp.tile(b_sp[:, :length].T, (reps, 1))
    return w.reshape(A_GROUPS * CHUNK, CHUNK).astype(BF16), jnp.repeat(bias, A_GROUP_DIM, axis=1)


def kernel(x_prompt, x_sample, cache_k, cache_v, page_table, norm1_g, w_in, b_gate, ln_v_g, ln_v_b, w_sp, b_sp,
           qn_g, kn_g, w_pa, w_pb, w_o, norm2_g, w_rg, b_rg, w_re, b_re, w1, w3, w2):
    bsz, seq, _ = x_prompt.shape
    db, n_new, _ = x_sample.shape
    past_len = page_table.shape[1] * PAGE_SIZE
    prm = _prepare(norm1_g, w_in, b_gate, ln_v_g, ln_v_b, qn_g, kn_g, w_pa, w_pb, w_o, norm2_g,
                   w_rg, b_rg, w_re, b_re)

    xp = x_prompt.reshape(bsz * seq, D_MODEL)
    wst, bsp = _spatial_weights(w_sp, b_sp, CHUNK)
    pa, gb, k_p, qt, kt, vt, kmean = _mixer(xp, prm, wst, bsp, True, seq)
    yb = _moba_prompt(qt, k_p, vt, kmean.reshape(-1, B_WIDTH), bsz, seq)
    h, xn, route = _merge(xp, pa, gb, yb, prm)
    y_p = _moe(xn, h, route, w1, w3, w2)

    xs = x_sample.reshape(db * n_new, D_MODEL)
    wst, bsp = _spatial_weights(w_sp, b_sp, n_new)
    pa, gb, q_s, k_s, v_s, va_s = _mixer(xs, prm, wst, bsp, False)
    n_phys = cache_k.shape[0]
    ckt = jnp.transpose(cache_k, (0, 2, 3, 1)).reshape(n_phys, B_WIDTH, PAGE_SIZE)
    cvt = jnp.transpose(cache_v, (0, 2, 3, 1)).reshape(n_phys, B_WIDTH, PAGE_SIZE)
    tok3 = lambda a: a.reshape(db, n_new, B_WIDTH)
    yb = _moba_sample(tok3(q_s), tok3(k_s), tok3(v_s), ckt, cvt, page_table, past_len)
    h, xn, route = _merge(xs, pa, gb, yb.reshape(db * n_new, B_WIDTH), prm)
    y_s = _moe(xn, h, route, w1, w3, w2)

    heads = lambda a, lead: a.reshape(*lead, B_HEADS, HEAD_DIM)
    from_t = lambda a: jnp.transpose(a.reshape(bsz, B_HEADS, HEAD_DIM, seq), (0, 3, 1, 2))
    return (y_p.reshape(bsz, seq, D_MODEL), y_s.reshape(db, n_new, D_MODEL),
            from_t(kt), from_t(vt),
            heads(k_s, (db, n_new)), heads(v_s, (db, n_new)),
            va_s.reshape(db, n_new, A_WIDTH))
```

```python
import functools

import jax
import jax.numpy as jnp
from jax import lax
from jax.experimental import pallas as pl
from jax.experimental.pallas import tpu as pltpu

F32 = jnp.float32
BF16 = jnp.bfloat16

D_MODEL = 1024
A_GROUPS = 8
A_GROUP_DIM = 64
A_WIDTH = A_GROUPS * A_GROUP_DIM
CHUNK = 128
B_HEADS = 8
HEAD_DIM = 64
B_WIDTH = B_HEADS * HEAD_DIM
MOBA_BLOCK = 256
MOBA_TOPK = 3
PAGE_SIZE = 128
SCALE = HEAD_DIM ** -0.5
MOE_GROUPS = 4
MOE_PER_GROUP = 4
N_EXPERTS = MOE_GROUPS * MOE_PER_GROUP
D_EXPERT = 512
EPS = 1e-6
OFF_U = 0
OFF_VA = OFF_U + A_WIDTH
OFF_Q = OFF_VA + A_WIDTH
OFF_K = OFF_Q + B_WIDTH
OFF_VB = OFF_K + B_WIDTH
OFF_GATE = OFF_VB + B_WIDTH

LANES = 128
MXU_COLS = 256
ROW_TILE = 256
PAIR = 2 * HEAD_DIM
VMEM_LIMIT = 56 * 1024 * 1024
NEG = -1e30


def _dot(a, b):
    return jnp.dot(a, b, preferred_element_type=F32)


def _dot_t(a, b):
    return lax.dot_general(a, b, (((1,), (1,)), ((), ())), preferred_element_type=F32)


def _split(a):
    hi = a.astype(BF16)
    lo = (a - hi.astype(F32)).astype(BF16)
    return hi, lo


def _dot3(a, b):
    ah, al = _split(a)
    bh, bl = _split(b)
    return _dot(ah, bh) + _dot(ah, bl) + _dot(al, bh)


def _mixer_kernel(prompt, x_ref, g1_ref, win_ref, bgate_ref, lng_ref, lnb_ref, seg_ref, wst_ref, bsp_ref,
                  qg_ref, kg_ref, wpa_ref, *outs):
    if prompt:
        pa_ref, gb_ref, k_ref, qt_ref, kt_ref, vt_ref, km_ref = outs
    else:
        pa_ref, gb_ref, q_ref, k_ref, v_ref, va_ref = outs
    x = x_ref[...]
    ms = jnp.mean(x * x, axis=-1, keepdims=True)
    xb = ((x * lax.rsqrt(ms + EPS)) * g1_ref[...]).astype(BF16)
    seg = seg_ref[...]

    def proj(off, width):
        return _dot(xb, win_ref[:, off:off + width])

    def segmean(a):
        return _dot(a.astype(BF16), seg)

    u = jax.nn.gelu(proj(OFF_U, A_WIDTH))
    va = jax.nn.gelu(proj(OFF_VA, A_WIDTH))
    xc = va - segmean(va)
    va = xc * lax.rsqrt(segmean(xc * xc) + EPS) * lng_ref[...] + lnb_ref[...]
    if not prompt:
        va_ref[...] = va

    gpt = MXU_COLS // A_GROUP_DIM
    grp = lax.broadcasted_iota(jnp.int32, (CHUNK, MXU_COLS), 1) // A_GROUP_DIM
    vb = va.astype(BF16)
    ya = []
    for c in range(ROW_TILE // CHUNK):
        rows = slice(c * CHUNK, (c + 1) * CHUNK)
        mixed = []
        for t in range(A_WIDTH // MXU_COLS):
            vt = vb[rows, t * MXU_COLS:(t + 1) * MXU_COLS]
            stack = jnp.concatenate([jnp.where(grp == g, vt, jnp.zeros_like(vt)) for g in range(gpt)], axis=0)
            mixed.append(_dot(wst_ref[:, t * gpt * CHUNK:(t + 1) * gpt * CHUNK], stack))
        ya.append(u[rows] * (jnp.concatenate(mixed, axis=1) + bsp_ref[...]))
    ya = jnp.concatenate(ya, axis=0)

    gates_a = jax.nn.sigmoid(proj(OFF_GATE, D_MODEL) + bgate_ref[:, :D_MODEL])
    pa_ref[...] = gates_a * _dot(ya.astype(BF16), wpa_ref[...])
    gb_ref[...] = jax.nn.sigmoid(proj(OFF_GATE + D_MODEL, D_MODEL) + bgate_ref[:, D_MODEL:])

    zq = proj(OFF_Q, B_WIDTH)
    q = zq * lax.rsqrt(segmean(zq * zq) + EPS) * qg_ref[...]
    zk = proj(OFF_K, B_WIDTH)
    k = zk * lax.rsqrt(segmean(zk * zk) + EPS) * kg_ref[...]
    v = proj(OFF_VB, B_WIDTH)
    k_ref[...] = k
    if prompt:
        qt_ref[...] = q.T
        kt_ref[...] = k.T
        vt_ref[...] = v.T
        km_ref[0] = jnp.mean(k, axis=0, keepdims=True)
    else:
        v_ref[...] = v
        q_ref[...] = q


def _mixer(x2d, prm, wst, bsp, prompt, seq=None):
    n = x2d.shape[0]
    assert n % ROW_TILE == 0
    nt = n // ROW_TILE
    full = lambda a: pl.BlockSpec(a.shape, lambda i: (0,) * a.ndim)
    rows = lambda w: pl.BlockSpec((ROW_TILE, w), lambda i: (i, 0))
    consts = [prm['g1'], prm['win'], prm['bgate'], prm['lng'], prm['lnb'], prm['seg'], wst, bsp,
              prm['qg'], prm['kg'], prm['wpa']]
    out_shape = [jax.ShapeDtypeStruct((n, D_MODEL), F32), jax.ShapeDtypeStruct((n, D_MODEL), F32)]
    out_specs = [rows(D_MODEL), rows(D_MODEL)]
    if prompt:
        tps = seq // ROW_TILE
        out_shape += [jax.ShapeDtypeStruct((n, B_WIDTH), F32)]
        out_specs += [rows(B_WIDTH)]
        out_shape += [jax.ShapeDtypeStruct((n // seq, B_WIDTH, seq), F32)] * 3
        out_specs += [pl.BlockSpec((None, B_WIDTH, ROW_TILE), lambda i: (i // tps, 0, i % tps))] * 3
        out_shape += [jax.ShapeDtypeStruct((nt, 1, B_WIDTH), F32)]
        out_specs += [pl.BlockSpec((1, 1, B_WIDTH), lambda i: (i, 0, 0))]
    else:
        out_shape += [jax.ShapeDtypeStruct((n, B_WIDTH), F32)] * 3
        out_specs += [rows(B_WIDTH)] * 3
        out_shape += [jax.ShapeDtypeStruct((n, A_WIDTH), F32)]
        out_specs += [rows(A_WIDTH)]
    return pl.pallas_call(
        functools.partial(_mixer_kernel, prompt),
        grid=(nt,),
        in_specs=[rows(D_MODEL)] + [full(a) for a in consts],
        out_specs=out_specs,
        out_shape=out_shape,
        compiler_params=pltpu.CompilerParams(dimension_semantics=("arbitrary",), vmem_limit_bytes=VMEM_LIMIT),
        name="mixer_p" if prompt else "mixer_s",
    )(x2d, *consts)


LOG2E = 1.4426950408889634


def _moba_prompt_kernel(n_sel, qt_ref, k_ref, vt_ref, km_ref, o_ref, bias_ref):
    i = pl.program_id(2)
    blk = MOBA_BLOCK
    nb = km_ref.shape[0]
    qt2 = qt_ref[...]
    km2 = km_ref[...]
    row = lax.broadcasted_iota(jnp.int32, (PAIR, blk), 0)
    brow = lax.broadcasted_iota(jnp.int32, (nb, blk), 0)
    hrows = [slice(hh * HEAD_DIM, (hh + 1) * HEAD_DIM) for hh in range(2)]
    qs = []
    for hh in range(2):
        qm = jnp.where((row // HEAD_DIM) == hh, qt2, 0.0)
        gate = jnp.where(brow < i, _dot3(km2, qm), -jnp.inf)
        bias = jnp.full((nb, blk), NEG, F32)
        for r in range(n_sel):
            mx = jnp.max(gate, axis=0, keepdims=True)
            idx = jnp.min(jnp.where(gate == mx, brow, nb), axis=0, keepdims=True)
            hit = brow == idx
            bias = jnp.where(hit & (r < i), 0.0, bias)
            gate = jnp.where(hit, -jnp.inf, gate)
        bias_ref[hh] = bias
        qs.append((qm * (SCALE * LOG2E)).astype(BF16))

    own = pl.multiple_of(i * blk, blk)
    kb = k_ref[pl.ds(own, blk), :].astype(BF16)
    kpos = lax.broadcasted_iota(jnp.int32, (blk, blk), 0)
    qpos = lax.broadcasted_iota(jnp.int32, (blk, blk), 1)
    s_own = [_dot(kb, qs[hh]) for hh in range(2)]
    state = []
    for hh in range(2):
        s = jnp.where(kpos <= qpos, s_own[hh], NEG)
        m = jnp.max(s, axis=0, keepdims=True)
        p = jnp.exp2(s - m)
        l = jnp.sum(p, axis=0, keepdims=True)
        acc = _dot(vt_ref[hrows[hh], pl.ds(own, blk)].astype(BF16), p.astype(BF16))
        state.append((m, l, acc))

    def body(c, carry):
        off = pl.multiple_of(c * (2 * blk), 2 * blk)
        kc = k_ref[pl.ds(off, 2 * blk), :].astype(BF16)
        s2 = [_dot(kc, qs[hh]) for hh in range(2)]
        soft = []
        for hh in range(2):
            m, l, acc = carry[hh]
            s_lo = s2[hh][:blk] + bias_ref[hh, pl.ds(2 * c, 1), :]
            s_hi = s2[hh][blk:] + bias_ref[hh, pl.ds(2 * c + 1, 1), :]
            m_new = jnp.maximum(m, jnp.maximum(jnp.max(s_lo, axis=0, keepdims=True),
                                               jnp.max(s_hi, axis=0, keepdims=True)))
            alpha = jnp.exp2(m - m_new)
            p_lo = jnp.exp2(s_lo - m_new)
            p_hi = jnp.exp2(s_hi - m_new)
            l = alpha * l + jnp.sum(p_lo, axis=0, keepdims=True) + jnp.sum(p_hi, axis=0, keepdims=True)
            soft.append((m_new, l, alpha, jnp.concatenate([p_lo, p_hi], axis=0).astype(BF16)))
        new = []
        for hh in range(2):
            m_new, l, alpha, p = soft[hh]
            acc = alpha * carry[hh][2] + _dot(vt_ref[hrows[hh], pl.ds(off, 2 * blk)].astype(BF16), p)
            new.append((m_new, l, acc))
        return tuple(new)

    state = lax.fori_loop(0, (i + 1) // 2, body, tuple(state))
    o_ref[...] = jnp.concatenate([acc / l for (_, l, acc) in state], axis=0).T


def _moba_prompt(qt, k, vt, kmean, bsz, seq):
    nb = seq // MOBA_BLOCK
    assert nb % 2 == 0
    n_sel = min(MOBA_TOPK, nb - 1)
    npair = B_WIDTH // PAIR
    return pl.pallas_call(
        functools.partial(_moba_prompt_kernel, n_sel),
        grid=(bsz, npair, nb),
        in_specs=[
            pl.BlockSpec((None, PAIR, MOBA_BLOCK), lambda b, p, i: (b, p, i)),
            pl.BlockSpec((seq, PAIR), lambda b, p, i: (b, p)),
            pl.BlockSpec((None, PAIR, seq), lambda b, p, i: (b, p, 0)),
            pl.BlockSpec((nb, PAIR), lambda b, p, i: (b, p)),
        ],
        out_specs=pl.BlockSpec((MOBA_BLOCK, PAIR), lambda b, p, i: (b * nb + i, p)),
        out_shape=jax.ShapeDtypeStruct((bsz * seq, B_WIDTH), F32),
        scratch_shapes=[pltpu.VMEM((2, nb, MOBA_BLOCK), F32)],
        compiler_params=pltpu.CompilerParams(dimension_semantics=("arbitrary",) * 3, vmem_limit_bytes=VMEM_LIMIT),
        name="moba_p",
    )(qt, k, vt, kmean)


SAMPLE_BLOCKS_PER_STEP = 8
PAGES_PER_BLOCK = MOBA_BLOCK // PAGE_SIZE


def _diag_fold(o_all):
    rows = o_all.shape[0]
    n_new = rows // B_HEADS
    rh = lax.broadcasted_iota(jnp.int32, (rows, B_WIDTH), 0) // n_new
    lh = lax.broadcasted_iota(jnp.int32, (rows, B_WIDTH), 1) // HEAD_DIM
    o = jnp.where(rh == lh, o_all, 0.0)
    out = o[:, 0:LANES]
    for t in range(1, B_WIDTH // LANES):
        out = out + o[:, t * LANES:(t + 1) * LANES]
    return out


def _moba_sample_kernel(n_new, nb_past, n_sel, pt_ref, q_ref, kn_ref, vn_ref, *rest):
    npg = SAMPLE_BLOCKS_PER_STEP * PAGES_PER_BLOCK
    k_refs = rest[:npg]
    v_refs = rest[npg:2 * npg]
    o_ref = rest[2 * npg]
    g_scr, m_scr, l_scr, o_scr = rest[2 * npg + 1:]
    step = pl.program_id(1)
    rows = B_HEADS * n_new
    rh = lax.broadcasted_iota(jnp.int32, (rows, B_WIDTH), 0) // n_new
    lh = lax.broadcasted_iota(jnp.int32, (rows, B_WIDTH), 1) // HEAD_DIM
    q = q_ref[...]
    qbd = jnp.where(rh == lh, jnp.concatenate([q] * B_HEADS, axis=0), 0.0).astype(BF16)
    lane = lax.broadcasted_iota(jnp.int32, (rows, LANES), 1)

    @pl.when(step == 0)
    def _():
        g_scr[...] = jnp.zeros_like(g_scr)
        m_scr[...] = jnp.zeros_like(m_scr)
        l_scr[...] = jnp.zeros_like(l_scr)

    s_pages = [_dot(qbd, k_refs[j][...].astype(BF16)) for j in range(npg)]
    probs, gates, maxes, sums = [], [], [], []
    for bi in range(SAMPLE_BLOCKS_PER_STEP):
        blk_pages = s_pages[bi * PAGES_PER_BLOCK:(bi + 1) * PAGES_PER_BLOCK]
        raw = blk_pages[0]
        mx = blk_pages[0]
        for sp in blk_pages[1:]:
            raw = raw + sp
            mx = jnp.maximum(mx, sp)
        gates.append(jnp.sum(raw, axis=1, keepdims=True) * (1.0 / MOBA_BLOCK))
        m_n = jnp.max(mx, axis=1, keepdims=True) * SCALE
        l_n = jnp.zeros((rows, 1), F32)
        for sp in blk_pages:
            p = jnp.exp(sp * SCALE - m_n)
            l_n = l_n + jnp.sum(p, axis=1, keepdims=True)
            probs.append(p.astype(BF16))
        maxes.append(m_n)
        sums.append(l_n)
    outs = []
    for bi in range(SAMPLE_BLOCKS_PER_STEP):
        o_all = _dot_t(probs[bi * PAGES_PER_BLOCK], v_refs[bi * PAGES_PER_BLOCK][...].astype(BF16))
        for pg in range(1, PAGES_PER_BLOCK):
            j = bi * PAGES_PER_BLOCK + pg
            o_all = o_all + _dot_t(probs[j], v_refs[j][...].astype(BF16))
        outs.append(o_all)
    g_new, m_new, l_new = g_scr[0:rows], m_scr[0:rows], l_scr[0:rows]
    for bi in range(SAMPLE_BLOCKS_PER_STEP):
        n = step * SAMPLE_BLOCKS_PER_STEP + bi
        o_scr[n] = _diag_fold(outs[bi])
        g_new = jnp.where(lane == n_new + n, gates[bi], g_new)
        m_new = jnp.where(lane == n_new + n, maxes[bi], m_new)
        l_new = jnp.where(lane == n_new + n, sums[bi], l_new)
    g_scr[0:rows] = g_new
    m_scr[0:rows] = m_new
    l_scr[0:rows] = l_new

    @pl.when(step == pl.num_programs(1) - 1)
    def _():
        ent = lax.broadcasted_iota(jnp.int32, (LANES, LANES), 0)
        col = lax.broadcasted_iota(jnp.int32, (LANES, LANES), 1)
        is_own = ent < n_new
        is_blk = (ent >= n_new) & (ent < n_new + nb_past)
        gate = jnp.where(is_blk, g_scr[...].T, -jnp.inf)
        sel = jnp.zeros((LANES, LANES), jnp.bool_)
        for _ in range(n_sel):
            mx = jnp.max(gate, axis=0, keepdims=True)
            idx = jnp.min(jnp.where(gate == mx, ent, LANES), axis=0, keepdims=True)
            hit = ent == idx
            sel = sel | hit
            gate = jnp.where(hit, -jnp.inf, gate)
        qpad = jnp.concatenate([qbd, jnp.zeros((LANES - rows, B_WIDTH), BF16)], axis=0)
        s_own = _dot_t(kn_ref[...].astype(BF16), qpad) * SCALE
        st = jnp.concatenate([s_own, m_scr[...].T[n_new:]], axis=0)
        valid = (is_own & (ent <= col % n_new)) | (is_blk & sel)
        st = jnp.where(valid, st, NEG)
        m = jnp.max(st, axis=0, keepdims=True)
        w = jnp.where(valid, jnp.exp(st - m), 0.0)
        den = jnp.sum(w * jnp.where(is_own, 1.0, l_scr[...].T), axis=0, keepdims=True)
        wn = (w / den).T[0:rows]
        num = _diag_fold(_dot(wn[:, 0:n_new].astype(BF16), vn_ref[...].astype(BF16)))
        for nblk in range(nb_past):
            num = num + wn[:, n_new + nblk:n_new + nblk + 1] * o_scr[nblk]
        out = jnp.where(rh == lh, jnp.concatenate([num] * (B_WIDTH // LANES), axis=1), 0.0)
        o_ref[...] = jnp.sum(out.reshape(B_HEADS, n_new, B_WIDTH), axis=0)


def _moba_sample(q, k_new, v_new, cache_kt, cache_vt, page_table, past_len):
    db, n_new, _ = q.shape
    assert past_len % MOBA_BLOCK == 0
    nb_past = past_len // MOBA_BLOCK
    assert nb_past % SAMPLE_BLOCKS_PER_STEP == 0 and n_new % 8 == 0
    assert n_new + nb_past <= LANES and B_HEADS * n_new <= LANES
    n_sel = min(MOBA_TOPK, nb_past)
    nsteps = nb_past // SAMPLE_BLOCKS_PER_STEP
    npg = SAMPLE_BLOCKS_PER_STEP * PAGES_PER_BLOCK
    rows = B_HEADS * n_new

    def page_map(j, b, s, pt):
        return (pt[b, s * npg + j], 0, 0)

    page_specs = [pl.BlockSpec((None, B_WIDTH, PAGE_SIZE), functools.partial(page_map, j)) for j in range(npg)]
    tok_spec = pl.BlockSpec((None, n_new, B_WIDTH), lambda b, s, pt: (b, 0, 0))
    grid_spec = pltpu.PrefetchScalarGridSpec(
        num_scalar_prefetch=1,
        grid=(db, nsteps),
        in_specs=[tok_spec] * 3 + page_specs + page_specs,
        out_specs=tok_spec,
        scratch_shapes=[pltpu.VMEM((LANES, LANES), F32)] * 3 + [pltpu.VMEM((nb_past, rows, LANES), F32)],
    )
    return pl.pallas_call(
        functools.partial(_moba_sample_kernel, n_new, nb_past, n_sel),
        grid_spec=grid_spec,
        out_shape=jax.ShapeDtypeStruct((db, n_new, B_WIDTH), F32),
        compiler_params=pltpu.CompilerParams(dimension_semantics=("arbitrary",) * 2, vmem_limit_bytes=VMEM_LIMIT),
        name="moba_s",
    )(page_table, q, k_new, v_new, *([cache_kt] * npg), *([cache_vt] * npg))


def _merge_kernel(x_ref, pa_ref, gb_ref, yb_ref, wpb_ref, wo_ref, g2_ref, wr_ref, br_ref, h_ref, xn_ref, route_ref):
    mix = pa_ref[...] + gb_ref[...] * _dot(yb_ref[...].astype(BF16), wpb_ref[...])
    h = x_ref[...] + _dot(mix.astype(BF16), wo_ref[...])
    h_ref[...] = h
    ms = jnp.mean(h * h, axis=-1, keepdims=True)
    xn = (h * lax.rsqrt(ms + EPS)) * g2_ref[...]
    xn_ref[...] = xn
    logits = _dot3(xn, wr_ref[...]) + br_ref[...]
    lane = lax.broadcasted_iota(jnp.int32, logits.shape, 1)
    is_grp = (lane >= N_EXPERTS) & (lane < N_EXPERTS + MOE_GROUPS)
    gl = jnp.where(is_grp, logits, -jnp.inf)
    gmax = jnp.max(gl, axis=1, keepdims=True)
    grp = jnp.min(jnp.where(gl == gmax, lane, LANES), axis=1, keepdims=True) - N_EXPERTS
    p_grp = 1.0 / jnp.sum(jnp.exp(gl - gmax), axis=1, keepdims=True)
    el = jnp.where((lane // MOE_PER_GROUP == grp) & (lane < N_EXPERTS), logits, -jnp.inf)
    v1 = jnp.max(el, axis=1, keepdims=True)
    i1 = jnp.min(jnp.where(el == v1, lane, LANES), axis=1, keepdims=True)
    el2 = jnp.where(lane == i1, -jnp.inf, el)
    v2 = jnp.max(el2, axis=1, keepdims=True)
    i2 = jnp.min(jnp.where(el2 == v2, lane, LANES), axis=1, keepdims=True)
    e2 = jnp.exp(v2 - v1)
    w1 = p_grp / (1.0 + e2)
    w2 = p_grp * e2 / (1.0 + e2)
    route = jnp.where(lane == 0, i1.astype(F32), jnp.where(lane == 1, i2.astype(F32), 0.0))
    route_ref[...] = jnp.where(lane == 2, w1, jnp.where(lane == 3, w2, route))


def _merge(x2d, pa, gb, yb, prm):
    n = x2d.shape[0]
    nt = n // ROW_TILE
    full = lambda a: pl.BlockSpec(a.shape, lambda i: (0,) * a.ndim)
    rows = lambda w: pl.BlockSpec((ROW_TILE, w), lambda i: (i, 0))
    consts = [prm['wpb'], prm['wo'], prm['g2'], prm['wr'], prm['br']]
    return pl.pallas_call(
        _merge_kernel,
        grid=(nt,),
        in_specs=[rows(D_MODEL), rows(D_MODEL), rows(D_MODEL), rows(B_WIDTH)] + [full(a) for a in consts],
        out_specs=[rows(D_MODEL), rows(D_MODEL), rows(LANES)],
        out_shape=[jax.ShapeDtypeStruct((n, D_MODEL), F32), jax.ShapeDtypeStruct((n, D_MODEL), F32),
                   jax.ShapeDtypeStruct((n, LANES), F32)],
        compiler_params=pltpu.CompilerParams(dimension_semantics=("arbitrary",), vmem_limit_bytes=VMEM_LIMIT),
        name="merge",
    )(x2d, pa, gb, yb, *consts)


EXPERT_TILE = 512
POS_ROWS = 8
DMA_UNROLL = 8


def _rank_kernel(route_ref, pos_ref, cnt_ref, carry_ref):
    @pl.when(pl.program_id(0) == 0)
    def _():
        carry_ref[...] = jnp.zeros_like(carry_ref)

    route = route_ref[...]
    tm = route.shape[0]
    lane = lax.broadcasted_iota(jnp.int32, (tm, LANES), 1)
    lanef = lane.astype(F32)
    i1, i2 = route[:, 0:1], route[:, 1:2]
    hit1, hit2 = lanef == i1, lanef == i2
    onehot = jnp.where(hit1 | hit2, 1.0, 0.0)
    earlier = lax.broadcasted_iota(jnp.int32, (tm, tm), 1) < lax.broadcasted_iota(jnp.int32, (tm, tm), 0)
    before = _dot(jnp.where(earlier, 1.0, 0.0).astype(BF16), onehot.astype(BF16)) + carry_ref[...]
    r1 = jnp.sum(jnp.where(hit1, before, 0.0), axis=1, keepdims=True)
    r2 = jnp.sum(jnp.where(hit2, before, 0.0), axis=1, keepdims=True)
    carry_ref[...] = carry_ref[...] + jnp.sum(onehot, axis=0, keepdims=True)
    packed = jnp.where(lane == 0, i1, jnp.where(lane == 1, r1, jnp.where(lane == 2, i2, jnp.where(lane == 3, r2, 0.0))))
    pos_ref[...] = packed.T[0:POS_ROWS].astype(jnp.int32)
    cnt_ref[...] = carry_ref[...].astype(jnp.int32)


def _dispatch_kernel(off_ref, cnt_ref, pos_ref, xn_ref, xs_hbm, zero_ref, zsem, sem):
    i = pl.program_id(0)
    tm = pos_ref.shape[1]

    last = N_EXPERTS - 1
    tg = zero_ref.shape[0]
    n_act = off_ref[last] // tg + (cnt_ref[last] + tg - 1) // tg
    kmax = xs_hbm.shape[0] // tg
    fills = [(cnt_ref[e] % tg != 0, off_ref[e] + (cnt_ref[e] // tg) * tg) for e in range(N_EXPERTS)]
    fills += [(n_act + j < kmax, (n_act + j) * tg) for j in range(N_EXPERTS)]

    def zero_copy(first):
        return pltpu.make_async_copy(zero_ref, xs_hbm.at[pl.ds(pl.multiple_of(first, tg), tg)], zsem)

    @pl.when(i == 0)
    def _():
        zero_ref[...] = jnp.zeros_like(zero_ref)
        for needed, first in fills:
            @pl.when(needed)
            def _():
                zero_copy(first).start()
        for needed, first in fills:
            @pl.when(needed)
            def _():
                zero_copy(first).wait()

    def row_copies(r):
        src = xn_ref.at[pl.ds(r, 1)]
        return [pltpu.make_async_copy(src, xs_hbm.at[pl.ds(pos_ref[s, r], 1)], sem)
                for s in range(2)]

    def issue(r, c):
        for cp in row_copies(r):
            cp.start()
        return c

    def drain(r, c):
        for cp in row_copies(r):
            cp.wait()
        return c

    lax.fori_loop(0, tm, issue, 0, unroll=DMA_UNROLL)
    lax.fori_loop(0, tm, drain, 0, unroll=DMA_UNROLL)


def _ffn_kernel(te_ref, nact_ref, xs_ref, w1_ref, w3_ref, w2_ref, ys_ref):
    active = pl.program_id(0) < nact_ref[0]

    @pl.when(active)
    def _():
        x = xs_ref[...].astype(BF16)
        hid = jax.nn.silu(_dot(x, w1_ref[...].astype(BF16))) * _dot(x, w3_ref[...].astype(BF16))
        ys_ref[...] = _dot(hid.astype(BF16), w2_ref[...].astype(BF16))

    @pl.when(jnp.logical_not(active))
    def _():
        ys_ref[...] = jnp.zeros_like(ys_ref)


def _combine_kernel(pos_ref, h_ref, route_ref, ys_hbm, o_ref, g1_ref, g2_ref, sem):
    tm = pos_ref.shape[1]
    bufs = (g1_ref, g2_ref)

    def row_copies(r):
        return [pltpu.make_async_copy(ys_hbm.at[pl.ds(pos_ref[s, r], 1)],
                                      bufs[s].at[pl.ds(r, 1)], sem) for s in range(2)]

    def issue(r, c):
        for cp in row_copies(r):
            cp.start()
        return c

    def drain(r, c):
        for cp in row_copies(r):
            cp.wait()
        return c

    lax.fori_loop(0, tm, issue, 0, unroll=DMA_UNROLL)
    lax.fori_loop(0, tm, drain, 0, unroll=DMA_UNROLL)
    route = route_ref[...]
    o_ref[...] = h_ref[...] + route[:, 2:3] * g1_ref[...] + route[:, 3:4] * g2_ref[...]


def _moe(xn, h, route, w1, w3, w2):
    n = xn.shape[0]
    tm = ROW_TILE
    nt = n // tm
    tg = min(EXPERT_TILE, max(ROW_TILE, n // N_EXPERTS))
    kmax = 2 * n // tg + N_EXPERTS
    cparams = pltpu.CompilerParams(dimension_semantics=("arbitrary",), vmem_limit_bytes=VMEM_LIMIT)

    pos, cnt = pl.pallas_call(
        _rank_kernel,
        grid=(nt,),
        in_specs=[pl.BlockSpec((tm, LANES), lambda i: (i, 0))],
        out_specs=[pl.BlockSpec((None, POS_ROWS, tm), lambda i: (i, 0, 0)), pl.BlockSpec((1, LANES), lambda i: (0, 0))],
        out_shape=[jax.ShapeDtypeStruct((nt, POS_ROWS, tm), jnp.int32), jax.ShapeDtypeStruct((1, LANES), jnp.int32)],
        scratch_shapes=[pltpu.VMEM((1, LANES), F32)],
        compiler_params=cparams, name="moe_rank",
    )(route)

    cnt = cnt[0, :N_EXPERTS]
    ntile = (cnt + tg - 1) // tg
    cum = jnp.cumsum(ntile)
    off = ((cum - ntile) * tg).astype(jnp.int32)
    nact = cum[-1:].astype(jnp.int32)
    tile = jnp.minimum(jnp.arange(kmax, dtype=jnp.int32), nact[0] - 1)
    tile_e = jnp.sum(tile[:, None] >= cum[None, :], axis=1).astype(jnp.int32)
    seg_off = lambda e: jnp.sum(jnp.where(e[..., None] == jnp.arange(N_EXPERTS), off, 0), axis=-1)
    dest = jnp.concatenate([seg_off(pos[:, 0:1]) + pos[:, 1:2], seg_off(pos[:, 2:3]) + pos[:, 3:4], pos[:, 2:]], axis=1)

    pos_spec = pl.BlockSpec((None, POS_ROWS, tm), lambda i, *_: (i, 0, 0), memory_space=pltpu.SMEM)
    any_spec = pl.BlockSpec(memory_space=pl.ANY)
    xs = pl.pallas_call(
        _dispatch_kernel,
        grid_spec=pltpu.PrefetchScalarGridSpec(
            num_scalar_prefetch=2, grid=(nt,),
            in_specs=[pos_spec, pl.BlockSpec((tm, D_MODEL), lambda i, *_: (i, 0))], out_specs=any_spec,
            scratch_shapes=[pltpu.VMEM((tg, D_MODEL), F32), pltpu.SemaphoreType.DMA, pltpu.SemaphoreType.DMA]),
        out_shape=jax.ShapeDtypeStruct((kmax * tg, D_MODEL), F32),
        compiler_params=cparams, name="moe_dispatch",
    )(off, cnt, dest, xn)

    act = lambda k, te, na: (jnp.minimum(k, na[0] - 1), 0)
    ys = pl.pallas_call(
        _ffn_kernel,
        grid_spec=pltpu.PrefetchScalarGridSpec(
            num_scalar_prefetch=2, grid=(kmax,),
            in_specs=[pl.BlockSpec((tg, D_MODEL), act),
                      pl.BlockSpec((None, D_MODEL, D_EXPERT), lambda k, te, na: (te[k], 0, 0)),
                      pl.BlockSpec((None, D_MODEL, D_EXPERT), lambda k, te, na: (te[k], 0, 0)),
                      pl.BlockSpec((None, D_EXPERT, D_MODEL), lambda k, te, na: (te[k], 0, 0))],
            out_specs=pl.BlockSpec((tg, D_MODEL), lambda k, te, na: (k, 0))),
        out_shape=jax.ShapeDtypeStruct((kmax * tg, D_MODEL), F32),
        compiler_params=cparams, name="moe_ffn",
    )(tile_e, nact, xs, w1, w3, w2)

    rows = lambda w: pl.BlockSpec((tm, w), lambda i, *_: (i, 0))
    return pl.pallas_call(
        _combine_kernel,
        grid_spec=pltpu.PrefetchScalarGridSpec(
            num_scalar_prefetch=0, grid=(nt,),
            in_specs=[pos_spec, rows(D_MODEL), rows(LANES), any_spec],
            out_specs=rows(D_MODEL),
            scratch_shapes=[pltpu.VMEM((tm, D_MODEL), F32), pltpu.VMEM((tm, D_MODEL), F32), pltpu.SemaphoreType.DMA]),
        out_shape=jax.ShapeDtypeStruct((n, D_MODEL), F32),
        compiler_params=cparams, name="moe_combine",
    )(dest, h, route, ys)


def _prepare(norm1_g, w_in, b_gate, ln_v_g, ln_v_b, qn_g, kn_g, w_pa, w_pb, w_o, norm2_g,
             w_rg, b_rg, w_re, b_re):
    seg = jnp.kron(jnp.eye(B_HEADS, dtype=F32), jnp.full((HEAD_DIM, HEAD_DIM), 1.0 / HEAD_DIM, F32))
    pad = LANES - N_EXPERTS - MOE_GROUPS
    return dict(
        g1=norm1_g.reshape(1, D_MODEL), win=w_in.astype(BF16), bgate=b_gate.reshape(1, 2 * D_MODEL),
        lng=ln_v_g.reshape(1, A_WIDTH), lnb=ln_v_b.reshape(1, A_WIDTH), seg=seg.astype(BF16),
        qg=jnp.tile(qn_g, B_HEADS).reshape(1, B_WIDTH), kg=jnp.tile(kn_g, B_HEADS).reshape(1, B_WIDTH),
        wpa=w_pa.astype(BF16), wpb=w_pb.astype(BF16), wo=w_o.astype(BF16), g2=norm2_g.reshape(1, D_MODEL),
        wr=jnp.pad(jnp.concatenate([w_re, w_rg], axis=1), ((0, 0), (0, pad))),
        br=jnp.pad(jnp.concatenate([b_re, b_rg]), (0, pad)).reshape(1, LANES),
    )


def _spatial_weights(w_sp, b_sp, length):
    reps = CHUNK // length
    tril = jnp.tril(jnp.ones((length, length), F32))
    w = w_sp[:, :length, :length] * tril
    w = jax.vmap(lambda m: jnp.kron(jnp.eye(reps, dtype=F32), m))(w)
    bias = jnp.tile(b_sp[:, :length].T, (reps, 1))
    side_by_side = jnp.transpose(w, (1, 0, 2)).reshape(CHUNK, A_GROUPS * CHUNK)
    return side_by_side.astype(BF16), jnp.repeat(bias, A_GROUP_DIM, axis=1)


def kernel(x_prompt, x_sample, cache_k, cache_v, page_table, norm1_g, w_in, b_gate, ln_v_g, ln_v_b, w_sp, b_sp,
           qn_g, kn_g, w_pa, w_pb, w_o, norm2_g, w_rg, b_rg, w_re, b_re, w1, w3, w2):
    bsz, seq, _ = x_prompt.shape
    db, n_new, _ = x_sample.shape
    past_len = page_table.shape[1] * PAGE_SIZE
    prm = _prepare(norm1_g, w_in, b_gate, ln_v_g, ln_v_b, qn_g, kn_g, w_pa, w_pb, w_o, norm2_g,
                   w_rg, b_rg, w_re, b_re)

    xp = x_prompt.reshape(bsz * seq, D_MODEL)
    wst, bsp = _spatial_weights(w_sp, b_sp, CHUNK)
    pa, gb, k_p, qt, kt, vt, kmean = _mixer(xp, prm, wst, bsp, True, seq)
    yb = _moba_prompt(qt, k_p, vt, kmean.reshape(-1, B_WIDTH), bsz, seq)
    h, xn, route = _merge(xp, pa, gb, yb, prm)
    y_p = _moe(xn, h, route, w1, w3, w2)

    xs = x_sample.reshape(db * n_new, D_MODEL)
    wst, bsp = _spatial_weights(w_sp, b_sp, n_new)
    pa, gb, q_s, k_s, v_s, va_s = _mixer(xs, prm, wst, bsp, False)
    n_phys = cache_k.shape[0]
    ckt = jnp.transpose(cache_k, (0, 2, 3, 1)).reshape(n_phys, B_WIDTH, PAGE_SIZE)
    cvt = jnp.transpose(cache_v, (0, 2, 3, 1)).reshape(n_phys, B_WIDTH, PAGE_SIZE)
    tok3 = lambda a: a.reshape(db, n_new, B_WIDTH)
    yb = _moba_sample(tok3(q_s), tok3(k_s), tok3(v_s), ckt, cvt, page_table, past_len)
    h, xn, route = _merge(xs, pa, gb, yb.reshape(db * n_new, B_WIDTH), prm)
    y_s = _moe(xn, h, route, w1, w3, w2)

    heads = lambda a, lead: a.reshape(*lead, B_HEADS, HEAD_DIM)
    from_t = lambda a: jnp.transpose(a.reshape(bsz, B_HEADS, HEAD_DIM, seq), (0, 3, 1, 2))
    return (y_p.reshape(bsz, seq, D_MODEL), y_s.reshape(db, n_new, D_MODEL),
            from_t(kt), from_t(vt),
            heads(k_s, (db, n_new)), heads(v_s, (db, n_new)),
            va_s.reshape(db, n_new, A_WIDTH))
```

```python
import functools

import jax
import jax.numpy as jnp
from jax import lax
from jax.experimental import pallas as pl
from jax.experimental.pallas import tpu as pltpu

F32 = jnp.float32
BF16 = jnp.bfloat16

D_MODEL = 1024
A_GROUPS = 8
A_GROUP_DIM = 64
A_WIDTH = A_GROUPS * A_GROUP_DIM
CHUNK = 128
B_HEADS = 8
HEAD_DIM = 64
B_WIDTH = B_HEADS * HEAD_DIM
MOBA_BLOCK = 256
MOBA_TOPK = 3
PAGE_SIZE = 128
SCALE = HEAD_DIM ** -0.5
MOE_GROUPS = 4
MOE_PER_GROUP = 4
N_EXPERTS = MOE_GROUPS * MOE_PER_GROUP
D_EXPERT = 512
EPS = 1e-6
OFF_U = 0
OFF_VA = OFF_U + A_WIDTH
OFF_Q = OFF_VA + A_WIDTH
OFF_K = OFF_Q + B_WIDTH
OFF_VB = OFF_K + B_WIDTH
OFF_GATE = OFF_VB + B_WIDTH

LANES = 128
MXU_COLS = 256
ROW_TILE = 256
PAIR = 2 * HEAD_DIM
VMEM_LIMIT = 56 * 1024 * 1024
NEG = -1e30


def _dot(a, b):
    return jnp.dot(a, b, preferred_element_type=F32)


def _dot_t(a, b):
    return lax.dot_general(a, b, (((1,), (1,)), ((), ())), preferred_element_type=F32)


def _split(a):
    hi = a.astype(BF16)
    lo = (a - hi.astype(F32)).astype(BF16)
    return hi, lo


def _dot3(a, b):
    ah, al = _split(a)
    bh, bl = _split(b)
    return _dot(ah, bh) + _dot(ah, bl) + _dot(al, bh)


def _mixer_kernel(prompt, x_ref, g1_ref, win_ref, bgate_ref, lng_ref, lnb_ref, seg_ref, wst_ref, bsp_ref,
                  qg_ref, kg_ref, wpa_ref, *outs):
    if prompt:
        pa_ref, gb_ref, kb_ref, qt_ref, kt_ref, vt_ref, vtb_ref, km_ref = outs
    else:
        pa_ref, gb_ref, q_ref, k_ref, v_ref, va_ref = outs
    x = x_ref[...]
    ms = jnp.mean(x * x, axis=-1, keepdims=True)
    xb = ((x * lax.rsqrt(ms + EPS)) * g1_ref[...]).astype(BF16)
    seg = seg_ref[...]

    def proj(off, width):
        return _dot(xb, win_ref[:, off:off + width])

    def segmean(a):
        return _dot(a.astype(BF16), seg)

    u = jax.nn.gelu(proj(OFF_U, A_WIDTH))
    va = jax.nn.gelu(proj(OFF_VA, A_WIDTH))
    xc = va - segmean(va)
    va = xc * lax.rsqrt(segmean(xc * xc) + EPS) * lng_ref[...] + lnb_ref[...]
    if not prompt:
        va_ref[...] = va

    gpt = MXU_COLS // A_GROUP_DIM
    grp = lax.broadcasted_iota(jnp.int32, (CHUNK, MXU_COLS), 1) // A_GROUP_DIM
    vb = va.astype(BF16)
    ya = []
    for c in range(ROW_TILE // CHUNK):
        rows = slice(c * CHUNK, (c + 1) * CHUNK)
        mixed = []
        for t in range(A_WIDTH // MXU_COLS):
            vt = vb[rows, t * MXU_COLS:(t + 1) * MXU_COLS]
            stack = jnp.concatenate([jnp.where(grp == g, vt, jnp.zeros_like(vt)) for g in range(gpt)], axis=0)
            mixed.append(_dot(wst_ref[:, t * gpt * CHUNK:(t + 1) * gpt * CHUNK], stack))
        ya.append(u[rows] * (jnp.concatenate(mixed, axis=1) + bsp_ref[...]))
    ya = jnp.concatenate(ya, axis=0)

    gates_a = jax.nn.sigmoid(proj(OFF_GATE, D_MODEL) + bgate_ref[:, :D_MODEL])
    pa_ref[...] = gates_a * _dot(ya.astype(BF16), wpa_ref[...])
    gb_ref[...] = jax.nn.sigmoid(proj(OFF_GATE + D_MODEL, D_MODEL) + bgate_ref[:, D_MODEL:])

    zq = proj(OFF_Q, B_WIDTH)
    q = zq * lax.rsqrt(segmean(zq * zq) + EPS) * qg_ref[...]
    zk = proj(OFF_K, B_WIDTH)
    k = zk * lax.rsqrt(segmean(zk * zk) + EPS) * kg_ref[...]
    v = proj(OFF_VB, B_WIDTH)
    if prompt:
        vt = v.T
        kb_ref[...] = k.astype(BF16)
        qt_ref[...] = q.T
        kt_ref[...] = k.T
        vt_ref[...] = vt
        vtb_ref[...] = vt.astype(BF16)
        km_ref[0] = jnp.mean(k, axis=0, keepdims=True)
    else:
        k_ref[...] = k
        v_ref[...] = v
        q_ref[...] = q


def _mixer(x2d, prm, wst, bsp, prompt, seq=None):
    n = x2d.shape[0]
    assert n % ROW_TILE == 0
    nt = n // ROW_TILE
    full = lambda a: pl.BlockSpec(a.shape, lambda i: (0,) * a.ndim)
    rows = lambda w: pl.BlockSpec((ROW_TILE, w), lambda i: (i, 0))
    consts = [prm['g1'], prm['win'], prm['bgate'], prm['lng'], prm['lnb'], prm['seg'], wst, bsp,
              prm['qg'], prm['kg'], prm['wpa']]
    out_shape = [jax.ShapeDtypeStruct((n, D_MODEL), F32), jax.ShapeDtypeStruct((n, D_MODEL), F32)]
    out_specs = [rows(D_MODEL), rows(D_MODEL)]
    if prompt:
        tps = seq // ROW_TILE
        out_shape += [jax.ShapeDtypeStruct((n, B_WIDTH), BF16)]
        out_specs += [rows(B_WIDTH)]
        out_shape += [jax.ShapeDtypeStruct((n // seq, B_WIDTH, seq), dt) for dt in (F32, F32, F32, BF16)]
        out_specs += [pl.BlockSpec((None, B_WIDTH, ROW_TILE), lambda i: (i // tps, 0, i % tps))] * 4
        out_shape += [jax.ShapeDtypeStruct((nt, 1, B_WIDTH), F32)]
        out_specs += [pl.BlockSpec((1, 1, B_WIDTH), lambda i: (i, 0, 0))]
    else:
        out_shape += [jax.ShapeDtypeStruct((n, B_WIDTH), F32)] * 3
        out_specs += [rows(B_WIDTH)] * 3
        out_shape += [jax.ShapeDtypeStruct((n, A_WIDTH), F32)]
        out_specs += [rows(A_WIDTH)]
    return pl.pallas_call(
        functools.partial(_mixer_kernel, prompt),
        grid=(nt,),
        in_specs=[rows(D_MODEL)] + [full(a) for a in consts],
        out_specs=out_specs,
        out_shape=out_shape,
        compiler_params=pltpu.CompilerParams(dimension_semantics=("arbitrary",), vmem_limit_bytes=VMEM_LIMIT),
        name="mixer_p" if prompt else "mixer_s",
    )(x2d, *consts)


LOG2E = 1.4426950408889634


MOBA_HEADS_PER_STEP = 8


def _moba_prompt_kernel(n_sel, qt_ref, k_ref, vt_ref, km_ref, o_ref, bias_ref):
    i = pl.program_id(2)
    blk = MOBA_BLOCK
    nb = km_ref.shape[0]
    heads = range(MOBA_HEADS_PER_STEP)
    row = lax.broadcasted_iota(jnp.int32, (PAIR, blk), 0)
    brow = lax.broadcasted_iota(jnp.int32, (nb, blk), 0)
    lanes = [slice((hh // 2) * PAIR, (hh // 2 + 1) * PAIR) for hh in heads]
    hrows = [slice(hh * HEAD_DIM, (hh + 1) * HEAD_DIM) for hh in heads]
    qs = []
    for hh in heads:
        qm = jnp.where((row // HEAD_DIM) == hh % 2, qt_ref[lanes[hh], :], 0.0)
        gate = jnp.where(brow < i, _dot3(km_ref[:, lanes[hh]], qm), -jnp.inf)
        bias = jnp.full((nb, blk), NEG, F32)
        for r in range(n_sel):
            mx = jnp.max(gate, axis=0, keepdims=True)
            idx = jnp.min(jnp.where(gate == mx, brow, nb), axis=0, keepdims=True)
            hit = brow == idx
            bias = jnp.where(hit & (r < i), 0.0, bias)
            gate = jnp.where(hit, -jnp.inf, gate)
        bias_ref[hh] = bias
        qs.append((qm * (SCALE * LOG2E)).astype(BF16))

    own = pl.multiple_of(i * blk, blk)
    kpos = lax.broadcasted_iota(jnp.int32, (blk, blk), 0)
    qpos = lax.broadcasted_iota(jnp.int32, (blk, blk), 1)
    s_own = [_dot(k_ref[pl.ds(own, blk), lanes[hh]], qs[hh]) for hh in heads]
    state = []
    for hh in heads:
        s = jnp.where(kpos <= qpos, s_own[hh], NEG)
        m = jnp.max(s, axis=0, keepdims=True)
        p = jnp.exp2(s - m)
        l = jnp.sum(p, axis=0, keepdims=True)
        acc = _dot(vt_ref[hrows[hh], pl.ds(own, blk)], p.astype(BF16))
        state.append((m, l, acc))

    def body(c, carry):
        off = pl.multiple_of(c * (2 * blk), 2 * blk)
        s2 = [_dot(k_ref[pl.ds(off, 2 * blk), lanes[hh]], qs[hh]) for hh in heads]
        soft = []
        for hh in heads:
            m, l, acc = carry[hh]
            s_lo = s2[hh][:blk] + bias_ref[hh, pl.ds(2 * c, 1), :]
            s_hi = s2[hh][blk:] + bias_ref[hh, pl.ds(2 * c + 1, 1), :]
            m_new = jnp.maximum(m, jnp.maximum(jnp.max(s_lo, axis=0, keepdims=True),
                                               jnp.max(s_hi, axis=0, keepdims=True)))
            alpha = jnp.exp2(m - m_new)
            p_lo = jnp.exp2(s_lo - m_new)
            p_hi = jnp.exp2(s_hi - m_new)
            l = alpha * l + jnp.sum(p_lo, axis=0, keepdims=True) + jnp.sum(p_hi, axis=0, keepdims=True)
            soft.append((m_new, l, alpha, jnp.concatenate([p_lo, p_hi], axis=0).astype(BF16)))
        new = []
        for hh in heads:
            m_new, l, alpha, p = soft[hh]
            acc = alpha * carry[hh][2] + _dot(vt_ref[hrows[hh], pl.ds(off, 2 * blk)], p)
            new.append((m_new, l, acc))
        return tuple(new)

    state = lax.fori_loop(0, (i + 1) // 2, body, tuple(state))
    o_ref[...] = jnp.concatenate([acc / l for (_, l, acc) in state], axis=0).T


def _moba_prompt(qt, k, vt, kmean, bsz, seq):
    nb = seq // MOBA_BLOCK
    assert nb % 2 == 0
    n_sel = min(MOBA_TOPK, nb - 1)
    width = MOBA_HEADS_PER_STEP * HEAD_DIM
    return pl.pallas_call(
        functools.partial(_moba_prompt_kernel, n_sel),
        grid=(bsz, B_WIDTH // width, nb),
        in_specs=[
            pl.BlockSpec((None, width, MOBA_BLOCK), lambda b, p, i: (b, p, i)),
            pl.BlockSpec((seq, width), lambda b, p, i: (b, p)),
            pl.BlockSpec((None, width, seq), lambda b, p, i: (b, p, 0)),
            pl.BlockSpec((nb, width), lambda b, p, i: (b, p)),
        ],
        out_specs=pl.BlockSpec((MOBA_BLOCK, width), lambda b, p, i: (b * nb + i, p)),
        out_shape=jax.ShapeDtypeStruct((bsz * seq, B_WIDTH), F32),
        scratch_shapes=[pltpu.VMEM((MOBA_HEADS_PER_STEP, nb, MOBA_BLOCK), F32)],
        compiler_params=pltpu.CompilerParams(dimension_semantics=("arbitrary",) * 3, vmem_limit_bytes=VMEM_LIMIT),
        name="moba_p",
    )(qt, k, vt, kmean)


SAMPLE_BLOCKS_PER_STEP = 8
PAGES_PER_BLOCK = MOBA_BLOCK // PAGE_SIZE


def _diag_fold(o_all):
    rows = o_all.shape[0]
    n_new = rows // B_HEADS
    rh = lax.broadcasted_iota(jnp.int32, (rows, B_WIDTH), 0) // n_new
    lh = lax.broadcasted_iota(jnp.int32, (rows, B_WIDTH), 1) // HEAD_DIM
    o = jnp.where(rh == lh, o_all, 0.0)
    out = o[:, 0:LANES]
    for t in range(1, B_WIDTH // LANES):
        out = out + o[:, t * LANES:(t + 1) * LANES]
    return out


def _moba_sample_kernel(n_new, nb_past, n_sel, pt_ref, q_ref, kn_ref, vn_ref, *rest):
    npg = SAMPLE_BLOCKS_PER_STEP * PAGES_PER_BLOCK
    k_refs = rest[:npg]
    v_refs = rest[npg:2 * npg]
    o_ref = rest[2 * npg]
    g_scr, m_scr, l_scr, o_scr = rest[2 * npg + 1:]
    step = pl.program_id(1)
    rows = B_HEADS * n_new
    rh = lax.broadcasted_iota(jnp.int32, (rows, B_WIDTH), 0) // n_new
    lh = lax.broadcasted_iota(jnp.int32, (rows, B_WIDTH), 1) // HEAD_DIM
    q = q_ref[...]
    qbd = jnp.where(rh == lh, jnp.concatenate([q] * B_HEADS, axis=0), 0.0).astype(BF16)
    lane = lax.broadcasted_iota(jnp.int32, (rows, LANES), 1)

    @pl.when(step == 0)
    def _():
        g_scr[...] = jnp.zeros_like(g_scr)
        m_scr[...] = jnp.zeros_like(m_scr)
        l_scr[...] = jnp.zeros_like(l_scr)

    s_pages = [_dot(qbd, k_refs[j][...].astype(BF16)) for j in range(npg)]
    probs, gates, maxes, sums = [], [], [], []
    for bi in range(SAMPLE_BLOCKS_PER_STEP):
        blk_pages = s_pages[bi * PAGES_PER_BLOCK:(bi + 1) * PAGES_PER_BLOCK]
        raw = blk_pages[0]
        mx = blk_pages[0]
        for sp in blk_pages[1:]:
            raw = raw + sp
            mx = jnp.maximum(mx, sp)
        gates.append(jnp.sum(raw, axis=1, keepdims=True) * (1.0 / MOBA_BLOCK))
        m_n = jnp.max(mx, axis=1, keepdims=True) * SCALE
        l_n = jnp.zeros((rows, 1), F32)
        for sp in blk_pages:
            p = jnp.exp(sp * SCALE - m_n)
            l_n = l_n + jnp.sum(p, axis=1, keepdims=True)
            probs.append(p.astype(BF16))
        maxes.append(m_n)
        sums.append(l_n)
    outs = []
    for bi in range(SAMPLE_BLOCKS_PER_STEP):
        o_all = _dot_t(probs[bi * PAGES_PER_BLOCK], v_refs[bi * PAGES_PER_BLOCK][...].astype(BF16))
        for pg in range(1, PAGES_PER_BLOCK):
            j = bi * PAGES_PER_BLOCK + pg
            o_all = o_all + _dot_t(probs[j], v_refs[j][...].astype(BF16))
        outs.append(o_all)
    g_new, m_new, l_new = g_scr[0:rows], m_scr[0:rows], l_scr[0:rows]
    for bi in range(SAMPLE_BLOCKS_PER_STEP):
        n = step * SAMPLE_BLOCKS_PER_STEP + bi
        o_scr[n] = _diag_fold(outs[bi])
        g_new = jnp.where(lane == n_new + n, gates[bi], g_new)
        m_new = jnp.where(lane == n_new + n, maxes[bi], m_new)
        l_new = jnp.where(lane == n_new + n, sums[bi], l_new)
    g_scr[0:rows] = g_new
    m_scr[0:rows] = m_new
    l_scr[0:rows] = l_new

    @pl.when(step == pl.num_programs(1) - 1)
    def _():
        ent = lax.broadcasted_iota(jnp.int32, (LANES, LANES), 0)
        col = lax.broadcasted_iota(jnp.int32, (LANES, LANES), 1)
        is_own = ent < n_new
        is_blk = (ent >= n_new) & (ent < n_new + nb_past)
        gate = jnp.where(is_blk, g_scr[...].T, -jnp.inf)
        sel = jnp.zeros((LANES, LANES), jnp.bool_)
        for _ in range(n_sel):
            mx = jnp.max(gate, axis=0, keepdims=True)
            idx = jnp.min(jnp.where(gate == mx, ent, LANES), axis=0, keepdims=True)
            hit = ent == idx
            sel = sel | hit
            gate = jnp.where(hit, -jnp.inf, gate)
        qpad = jnp.concatenate([qbd, jnp.zeros((LANES - rows, B_WIDTH), BF16)], axis=0)
        s_own = _dot_t(kn_ref[...].astype(BF16), qpad) * SCALE
        st = jnp.concatenate([s_own, m_scr[...].T[n_new:]], axis=0)
        valid = (is_own & (ent <= col % n_new)) | (is_blk & sel)
        st = jnp.where(valid, st, NEG)
        m = jnp.max(st, axis=0, keepdims=True)
        w = jnp.where(valid, jnp.exp(st - m), 0.0)
        den = jnp.sum(w * jnp.where(is_own, 1.0, l_scr[...].T), axis=0, keepdims=True)
        wn = (w / den).T[0:rows]
        num = _diag_fold(_dot(wn[:, 0:n_new].astype(BF16), vn_ref[...].astype(BF16)))
        for nblk in range(nb_past):
            num = num + wn[:, n_new + nblk:n_new + nblk + 1] * o_scr[nblk]
        out = jnp.where(rh == lh, jnp.concatenate([num] * (B_WIDTH // LANES), axis=1), 0.0)
        o_ref[...] = jnp.sum(out.reshape(B_HEADS, n_new, B_WIDTH), axis=0)


def _moba_sample(q, k_new, v_new, cache_kt, cache_vt, page_table, past_len):
    db, n_new, _ = q.shape
    assert past_len % MOBA_BLOCK == 0
    nb_past = past_len // MOBA_BLOCK
    assert nb_past % SAMPLE_BLOCKS_PER_STEP == 0 and n_new % 8 == 0
    assert n_new + nb_past <= LANES and B_HEADS * n_new <= LANES
    n_sel = min(MOBA_TOPK, nb_past)
    nsteps = nb_past // SAMPLE_BLOCKS_PER_STEP
    npg = SAMPLE_BLOCKS_PER_STEP * PAGES_PER_BLOCK
    rows = B_HEADS * n_new

    def page_map(j, b, s, pt):
        return (pt[b, s * npg + j], 0, 0)

    page_specs = [pl.BlockSpec((None, B_WIDTH, PAGE_SIZE), functools.partial(page_map, j)) for j in range(npg)]
    tok_spec = pl.BlockSpec((None, n_new, B_WIDTH), lambda b, s, pt: (b, 0, 0))
    grid_spec = pltpu.PrefetchScalarGridSpec(
        num_scalar_prefetch=1,
        grid=(db, nsteps),
        in_specs=[tok_spec] * 3 + page_specs + page_specs,
        out_specs=tok_spec,
        scratch_shapes=[pltpu.VMEM((LANES, LANES), F32)] * 3 + [pltpu.VMEM((nb_past, rows, LANES), F32)],
    )
    return pl.pallas_call(
        functools.partial(_moba_sample_kernel, n_new, nb_past, n_sel),
        grid_spec=grid_spec,
        out_shape=jax.ShapeDtypeStruct((db, n_new, B_WIDTH), F32),
        compiler_params=pltpu.CompilerParams(dimension_semantics=("arbitrary",) * 2, vmem_limit_bytes=VMEM_LIMIT),
        name="moba_s",
    )(page_table, q, k_new, v_new, *([cache_kt] * npg), *([cache_vt] * npg))


def _merge_kernel(x_ref, pa_ref, gb_ref, yb_ref, wpb_ref, wo_ref, g2_ref, wr_ref, br_ref, h_ref, xn_ref, route_ref):
    mix = pa_ref[...] + gb_ref[...] * _dot(yb_ref[...].astype(BF16), wpb_ref[...])
    h = x_ref[...] + _dot(mix.astype(BF16), wo_ref[...])
    h_ref[...] = h
    ms = jnp.mean(h * h, axis=-1, keepdims=True)
    xn = (h * lax.rsqrt(ms + EPS)) * g2_ref[...]
    xn_ref[...] = xn
    logits = _dot3(xn, wr_ref[...]) + br_ref[...]
    lane = lax.broadcasted_iota(jnp.int32, logits.shape, 1)
    is_grp = (lane >= N_EXPERTS) & (lane < N_EXPERTS + MOE_GROUPS)
    gl = jnp.where(is_grp, logits, -jnp.inf)
    gmax = jnp.max(gl, axis=1, keepdims=True)
    grp = jnp.min(jnp.where(gl == gmax, lane, LANES), axis=1, keepdims=True) - N_EXPERTS
    p_grp = 1.0 / jnp.sum(jnp.exp(gl - gmax), axis=1, keepdims=True)
    el = jnp.where((lane // MOE_PER_GROUP == grp) & (lane < N_EXPERTS), logits, -jnp.inf)
    v1 = jnp.max(el, axis=1, keepdims=True)
    i1 = jnp.min(jnp.where(el == v1, lane, LANES), axis=1, keepdims=True)
    el2 = jnp.where(lane == i1, -jnp.inf, el)
    v2 = jnp.max(el2, axis=1, keepdims=True)
    i2 = jnp.min(jnp.where(el2 == v2, lane, LANES), axis=1, keepdims=True)
    e2 = jnp.exp(v2 - v1)
    w1 = p_grp / (1.0 + e2)
    w2 = p_grp * e2 / (1.0 + e2)
    route = jnp.where(lane == 0, i1.astype(F32), jnp.where(lane == 1, i2.astype(F32), 0.0))
    route_ref[...] = jnp.where(lane == 2, w1, jnp.where(lane == 3, w2, route))


def _merge(x2d, pa, gb, yb, prm):
    n = x2d.shape[0]
    nt = n // ROW_TILE
    full = lambda a: pl.BlockSpec(a.shape, lambda i: (0,) * a.ndim)
    rows = lambda w: pl.BlockSpec((ROW_TILE, w), lambda i: (i, 0))
    consts = [prm['wpb'], prm['wo'], prm['g2'], prm['wr'], prm['br']]
    return pl.pallas_call(
        _merge_kernel,
        grid=(nt,),
        in_specs=[rows(D_MODEL), rows(D_MODEL), rows(D_MODEL), rows(B_WIDTH)] + [full(a) for a in consts],
        out_specs=[rows(D_MODEL), rows(D_MODEL), rows(LANES)],
        out_shape=[jax.ShapeDtypeStruct((n, D_MODEL), F32), jax.ShapeDtypeStruct((n, D_MODEL), F32),
                   jax.ShapeDtypeStruct((n, LANES), F32)],
        compiler_params=pltpu.CompilerParams(dimension_semantics=("arbitrary",), vmem_limit_bytes=VMEM_LIMIT),
        name="merge",
    )(x2d, pa, gb, yb, *consts)


EXPERT_TILE = 512
POS_ROWS = 8
DMA_UNROLL = 8


def _rank_kernel(route_ref, pos_ref, cnt_ref, carry_ref):
    @pl.when(pl.program_id(0) == 0)
    def _():
        carry_ref[...] = jnp.zeros_like(carry_ref)

    route = route_ref[...]
    tm = route.shape[0]
    lane = lax.broadcasted_iota(jnp.int32, (tm, LANES), 1)
    lanef = lane.astype(F32)
    i1, i2 = route[:, 0:1], route[:, 1:2]
    hit1, hit2 = lanef == i1, lanef == i2
    onehot = jnp.where(hit1 | hit2, 1.0, 0.0)
    earlier = lax.broadcasted_iota(jnp.int32, (tm, tm), 1) < lax.broadcasted_iota(jnp.int32, (tm, tm), 0)
    before = _dot(jnp.where(earlier, 1.0, 0.0).astype(BF16), onehot.astype(BF16)) + carry_ref[...]
    r1 = jnp.sum(jnp.where(hit1, before, 0.0), axis=1, keepdims=True)
    r2 = jnp.sum(jnp.where(hit2, before, 0.0), axis=1, keepdims=True)
    carry_ref[...] = carry_ref[...] + jnp.sum(onehot, axis=0, keepdims=True)
    packed = jnp.where(lane == 0, i1, jnp.where(lane == 1, r1, jnp.where(lane == 2, i2, jnp.where(lane == 3, r2, 0.0))))
    pos_ref[...] = packed.T[0:POS_ROWS].astype(jnp.int32)
    cnt_ref[...] = carry_ref[...].astype(jnp.int32)


def _dispatch_kernel(off_ref, cnt_ref, pos_ref, xn_ref, xs_hbm, zero_ref, zsem, sem):
    i = pl.program_id(0)
    tm = pos_ref.shape[1]

    last = N_EXPERTS - 1
    tg = zero_ref.shape[0]
    n_act = off_ref[last] // tg + (cnt_ref[last] + tg - 1) // tg
    kmax = xs_hbm.shape[0] // tg
    fills = [(cnt_ref[e] % tg != 0, off_ref[e] + (cnt_ref[e] // tg) * tg) for e in range(N_EXPERTS)]
    fills += [(n_act + j < kmax, (n_act + j) * tg) for j in range(N_EXPERTS)]

    def zero_copy(first):
        return pltpu.make_async_copy(zero_ref, xs_hbm.at[pl.ds(pl.multiple_of(first, tg), tg)], zsem)

    @pl.when(i == 0)
    def _():
        zero_ref[...] = jnp.zeros_like(zero_ref)
        for needed, first in fills:
            @pl.when(needed)
            def _():
                zero_copy(first).start()
        for needed, first in fills:
            @pl.when(needed)
            def _():
                zero_copy(first).wait()

    def row_copies(r):
        src = xn_ref.at[pl.ds(r, 1)]
        return [pltpu.make_async_copy(src, xs_hbm.at[pl.ds(pos_ref[s, r], 1)], sem)
                for s in range(2)]

    def issue(r, c):
        for cp in row_copies(r):
            cp.start()
        return c

    def drain(r, c):
        for cp in row_copies(r):
            cp.wait()
        return c

    lax.fori_loop(0, tm, issue, 0, unroll=DMA_UNROLL)
    lax.fori_loop(0, tm, drain, 0, unroll=DMA_UNROLL)


def _ffn_kernel(te_ref, nact_ref, xs_ref, w1_ref, w3_ref, w2_ref, ys_ref):
    active = pl.program_id(0) < nact_ref[0]

    @pl.when(active)
    def _():
        x = xs_ref[...].astype(BF16)
        hid = jax.nn.silu(_dot(x, w1_ref[...].astype(BF16))) * _dot(x, w3_ref[...].astype(BF16))
        ys_ref[...] = _dot(hid.astype(BF16), w2_ref[...].astype(BF16))

    @pl.when(jnp.logical_not(active))
    def _():
        ys_ref[...] = jnp.zeros_like(ys_ref)


def _combine_kernel(pos_ref, h_ref, route_ref, ys_hbm, o_ref, g1_ref, g2_ref, sem):
    tm = pos_ref.shape[1]
    bufs = (g1_ref, g2_ref)

    def row_copies(r):
        return [pltpu.make_async_copy(ys_hbm.at[pl.ds(pos_ref[s, r], 1)],
                                      bufs[s].at[pl.ds(r, 1)], sem) for s in range(2)]

    def issue(r, c):
        for cp in row_copies(r):
            cp.start()
        return c

    def drain(r, c):
        for cp in row_copies(r):
            cp.wait()
        return c

    lax.fori_loop(0, tm, issue, 0, unroll=DMA_UNROLL)
    lax.fori_loop(0, tm, drain, 0, unroll=DMA_UNROLL)
    route = route_ref[...]
    o_ref[...] = h_ref[...] + route[:, 2:3] * g1_ref[...] + route[:, 3:4] * g2_ref[...]


def _moe(xn, h, route, w1, w3, w2):
    n = xn.shape[0]
    tm = ROW_TILE
    nt = n // tm
    tg = min(EXPERT_TILE, max(ROW_TILE, n // N_EXPERTS))
    kmax = 2 * n // tg + N_EXPERTS
    cparams = pltpu.CompilerParams(dimension_semantics=("arbitrary",), vmem_limit_bytes=VMEM_LIMIT)

    pos, cnt = pl.pallas_call(
        _rank_kernel,
        grid=(nt,),
        in_specs=[pl.BlockSpec((tm, LANES), lambda i: (i, 0))],
        out_specs=[pl.BlockSpec((None, POS_ROWS, tm), lambda i: (i, 0, 0)), pl.BlockSpec((1, LANES), lambda i: (0, 0))],
        out_shape=[jax.ShapeDtypeStruct((nt, POS_ROWS, tm), jnp.int32), jax.ShapeDtypeStruct((1, LANES), jnp.int32)],
        scratch_shapes=[pltpu.VMEM((1, LANES), F32)],
        compiler_params=cparams, name="moe_rank",
    )(route)

    cnt = cnt[0, :N_EXPERTS]
    ntile = (cnt + tg - 1) // tg
    cum = jnp.cumsum(ntile)
    off = ((cum - ntile) * tg).astype(jnp.int32)
    nact = cum[-1:].astype(jnp.int32)
    tile = jnp.minimum(jnp.arange(kmax, dtype=jnp.int32), nact[0] - 1)
    tile_e = jnp.sum(tile[:, None] >= cum[None, :], axis=1).astype(jnp.int32)
    seg_off = lambda e: jnp.sum(jnp.where(e[..., None] == jnp.arange(N_EXPERTS), off, 0), axis=-1)
    dest = jnp.concatenate([seg_off(pos[:, 0:1]) + pos[:, 1:2], seg_off(pos[:, 2:3]) + pos[:, 3:4], pos[:, 2:]], axis=1)

    pos_spec = pl.BlockSpec((None, POS_ROWS, tm), lambda i, *_: (i, 0, 0), memory_space=pltpu.SMEM)
    any_spec = pl.BlockSpec(memory_space=pl.ANY)
    xs = pl.pallas_call(
        _dispatch_kernel,
        grid_spec=pltpu.PrefetchScalarGridSpec(
            num_scalar_prefetch=2, grid=(nt,),
            in_specs=[pos_spec, pl.BlockSpec((tm, D_MODEL), lambda i, *_: (i, 0))], out_specs=any_spec,
            scratch_shapes=[pltpu.VMEM((tg, D_MODEL), F32), pltpu.SemaphoreType.DMA, pltpu.SemaphoreType.DMA]),
        out_shape=jax.ShapeDtypeStruct((kmax * tg, D_MODEL), F32),
        compiler_params=cparams, name="moe_dispatch",
    )(off, cnt, dest, xn)

    act = lambda k, te, na: (jnp.minimum(k, na[0] - 1), 0)
    ys = pl.pallas_call(
        _ffn_kernel,
        grid_spec=pltpu.PrefetchScalarGridSpec(
            num_scalar_prefetch=2, grid=(kmax,),
            in_specs=[pl.BlockSpec((tg, D_MODEL), act),
                      pl.BlockSpec((None, D_MODEL, D_EXPERT), lambda k, te, na: (te[k], 0, 0)),
                      pl.BlockSpec((None, D_MODEL, D_EXPERT), lambda k, te, na: (te[k], 0, 0)),
                      pl.BlockSpec((None, D_EXPERT, D_MODEL), lambda k, te, na: (te[k], 0, 0))],
            out_specs=pl.BlockSpec((tg, D_MODEL), lambda k, te, na: (k, 0))),
        out_shape=jax.ShapeDtypeStruct((kmax * tg, D_MODEL), F32),
        compiler_params=cparams, name="moe_ffn",
    )(tile_e, nact, xs, w1, w3, w2)

    rows = lambda w: pl.BlockSpec((tm, w), lambda i, *_: (i, 0))
    return pl.pallas_call(
        _combine_kernel,
        grid_spec=pltpu.PrefetchScalarGridSpec(
            num_scalar_prefetch=0, grid=(nt,),
            in_specs=[pos_spec, rows(D_MODEL), rows(LANES), any_spec],
            out_specs=rows(D_MODEL),
            scratch_shapes=[pltpu.VMEM((tm, D_MODEL), F32), pltpu.VMEM((tm, D_MODEL), F32), pltpu.SemaphoreType.DMA]),
        out_shape=jax.ShapeDtypeStruct((n, D_MODEL), F32),
        compiler_params=cparams, name="moe_combine",
    )(dest, h, route, ys)


def _prepare(norm1_g, w_in, b_gate, ln_v_g, ln_v_b, qn_g, kn_g, w_pa, w_pb, w_o, norm2_g,
             w_rg, b_rg, w_re, b_re):
    seg = jnp.kron(jnp.eye(B_HEADS, dtype=F32), jnp.full((HEAD_DIM, HEAD_DIM), 1.0 / HEAD_DIM, F32))
    pad = LANES - N_EXPERTS - MOE_GROUPS
    return dict(
        g1=norm1_g.reshape(1, D_MODEL), win=w_in.astype(BF16), bgate=b_gate.reshape(1, 2 * D_MODEL),
        lng=ln_v_g.reshape(1, A_WIDTH), lnb=ln_v_b.reshape(1, A_WIDTH), seg=seg.astype(BF16),
        qg=jnp.tile(qn_g, B_HEADS).reshape(1, B_WIDTH), kg=jnp.tile(kn_g, B_HEADS).reshape(1, B_WIDTH),
        wpa=w_pa.astype(BF16), wpb=w_pb.astype(BF16), wo=w_o.astype(BF16), g2=norm2_g.reshape(1, D_MODEL),
        wr=jnp.pad(jnp.concatenate([w_re, w_rg], axis=1), ((0, 0), (0, pad))),
        br=jnp.pad(jnp.concatenate([b_re, b_rg]), (0, pad)).reshape(1, LANES),
    )


def _spatial_weights(w_sp, b_sp, length):
    reps = CHUNK // length
    tril = jnp.tril(jnp.ones((length, length), F32))
    w = w_sp[:, :length, :length] * tril
    w = jax.vmap(lambda m: jnp.kron(jnp.eye(reps, dtype=F32), m))(w)
    bias = jnp.tile(b_sp[:, :length].T, (reps, 1))
    side_by_side = jnp.transpose(w, (1, 0, 2)).reshape(CHUNK, A_GROUPS * CHUNK)
    return side_by_side.astype(BF16), jnp.repeat(bias, A_GROUP_DIM, axis=1)


def kernel(x_prompt, x_sample, cache_k, cache_v, page_table, norm1_g, w_in, b_gate, ln_v_g, ln_v_b, w_sp, b_sp,
           qn_g, kn_g, w_pa, w_pb, w_o, norm2_g, w_rg, b_rg, w_re, b_re, w1, w3, w2):
    bsz, seq, _ = x_prompt.shape
    db, n_new, _ = x_sample.shape
    past_len = page_table.shape[1] * PAGE_SIZE
    prm = _prepare(norm1_g, w_in, b_gate, ln_v_g, ln_v_b, qn_g, kn_g, w_pa, w_pb, w_o, norm2_g,
                   w_rg, b_rg, w_re, b_re)

    xp = x_prompt.reshape(bsz * seq, D_MODEL)
    wst, bsp = _spatial_weights(w_sp, b_sp, CHUNK)
    pa, gb, k_b, qt, kt, vt, vt_b, kmean = _mixer(xp, prm, wst, bsp, True, seq)
    yb = _moba_prompt(qt, k_b, vt_b, kmean.reshape(-1, B_WIDTH), bsz, seq)
    h, xn, route = _merge(xp, pa, gb, yb, prm)
    y_p = _moe(xn, h, route, w1, w3, w2)

    xs = x_sample.reshape(db * n_new, D_MODEL)
    wst, bsp = _spatial_weights(w_sp, b_sp, n_new)
    pa, gb, q_s, k_s, v_s, va_s = _mixer(xs, prm, wst, bsp, False)
    n_phys = cache_k.shape[0]
    ckt = jnp.transpose(cache_k, (0, 2, 3, 1)).reshape(n_phys, B_WIDTH, PAGE_SIZE)
    cvt = jnp.transpose(cache_v, (0, 2, 3, 1)).reshape(n_phys, B_WIDTH, PAGE_SIZE)
    tok3 = lambda a: a.reshape(db, n_new, B_WIDTH)
    yb = _moba_sample(tok3(q_s), tok3(k_s), tok3(v_s), ckt, cvt, page_table, past_len)
    h, xn, route = _merge(xs, pa, gb, yb.reshape(db * n_new, B_WIDTH), prm)
    y_s = _moe(xn, h, route, w1, w3, w2)

    heads = lambda a, lead: a.reshape(*lead, B_HEADS, HEAD_DIM)
    from_t = lambda a: jnp.transpose(a.reshape(bsz, B_HEADS, HEAD_DIM, seq), (0, 3, 1, 2))
    return (y_p.reshape(bsz, seq, D_MODEL), y_s.reshape(db, n_new, D_MODEL),
            from_t(kt), from_t(vt),
            heads(k_s, (db, n_new)), heads(v_s, (db, n_new)),
            va_s.reshape(db, n_new, A_WIDTH))
```

```python
import functools

import jax
import jax.numpy as jnp
from jax import lax
from jax.experimental import pallas as pl
from jax.experimental.pallas import tpu as pltpu

F32 = jnp.float32
BF16 = jnp.bfloat16

D_MODEL = 1024
A_GROUPS = 8
A_GROUP_DIM = 64
A_WIDTH = A_GROUPS * A_GROUP_DIM
CHUNK = 128
B_HEADS = 8
HEAD_DIM = 64
B_WIDTH = B_HEADS * HEAD_DIM
MOBA_BLOCK = 256
MOBA_TOPK = 3
PAGE_SIZE = 128
SCALE = HEAD_DIM ** -0.5
MOE_GROUPS = 4
MOE_PER_GROUP = 4
N_EXPERTS = MOE_GROUPS * MOE_PER_GROUP
D_EXPERT = 512
EPS = 1e-6
OFF_U = 0
OFF_VA = OFF_U + A_WIDTH
OFF_Q = OFF_VA + A_WIDTH
OFF_K = OFF_Q + B_WIDTH
OFF_VB = OFF_K + B_WIDTH
OFF_GATE = OFF_VB + B_WIDTH

LANES = 128
MXU_COLS = 256
ROW_TILE = 256
PAIR = 2 * HEAD_DIM
VMEM_LIMIT = 56 * 1024 * 1024
NEG = -1e30


def _dot(a, b):
    return jnp.dot(a, b, preferred_element_type=F32)


def _dot_t(a, b):
    return lax.dot_general(a, b, (((1,), (1,)), ((), ())), preferred_element_type=F32)


def _split(a):
    hi = a.astype(BF16)
    lo = (a - hi.astype(F32)).astype(BF16)
    return hi, lo


def _dot3(a, b):
    ah, al = _split(a)
    bh, bl = _split(b)
    return _dot(ah, bh) + _dot(ah, bl) + _dot(al, bh)


def _mixer_kernel(prompt, x_ref, g1_ref, win_ref, bgate_ref, lng_ref, lnb_ref, seg_ref, wst_ref, bsp_ref,
                  qg_ref, kg_ref, wpa_ref, *outs):
    if prompt:
        pa_ref, gb_ref, kb_ref, qt_ref, kt_ref, vt_ref, vtb_ref, km_ref = outs
    else:
        pa_ref, gb_ref, q_ref, k_ref, v_ref, va_ref = outs
    x = x_ref[...]
    ms = jnp.mean(x * x, axis=-1, keepdims=True)
    xb = ((x * lax.rsqrt(ms + EPS)) * g1_ref[...]).astype(BF16)
    seg = seg_ref[...]

    def proj(off, width):
        return _dot(xb, win_ref[:, off:off + width])

    def segmean(a):
        return _dot(a.astype(BF16), seg)

    u = jax.nn.gelu(proj(OFF_U, A_WIDTH))
    va = jax.nn.gelu(proj(OFF_VA, A_WIDTH))
    xc = va - segmean(va)
    va = xc * lax.rsqrt(segmean(xc * xc) + EPS) * lng_ref[...] + lnb_ref[...]
    if not prompt:
        va_ref[...] = va

    gpt = MXU_COLS // A_GROUP_DIM
    grp = lax.broadcasted_iota(jnp.int32, (CHUNK, MXU_COLS), 1) // A_GROUP_DIM
    vb = va.astype(BF16)
    ya = []
    for c in range(ROW_TILE // CHUNK):
        rows = slice(c * CHUNK, (c + 1) * CHUNK)
        mixed = []
        for t in range(A_WIDTH // MXU_COLS):
            vt = vb[rows, t * MXU_COLS:(t + 1) * MXU_COLS]
            stack = jnp.concatenate([jnp.where(grp == g, vt, jnp.zeros_like(vt)) for g in range(gpt)], axis=0)
            mixed.append(_dot(wst_ref[:, t * gpt * CHUNK:(t + 1) * gpt * CHUNK], stack))
        ya.append(u[rows] * (jnp.concatenate(mixed, axis=1) + bsp_ref[...]))
    ya = jnp.concatenate(ya, axis=0)

    gates_a = jax.nn.sigmoid(proj(OFF_GATE, D_MODEL) + bgate_ref[:, :D_MODEL])
    pa_ref[...] = gates_a * _dot(ya.astype(BF16), wpa_ref[...])
    gb_ref[...] = jax.nn.sigmoid(proj(OFF_GATE + D_MODEL, D_MODEL) + bgate_ref[:, D_MODEL:])

    zq = proj(OFF_Q, B_WIDTH)
    q = zq * lax.rsqrt(segmean(zq * zq) + EPS) * qg_ref[...]
    zk = proj(OFF_K, B_WIDTH)
    k = zk * lax.rsqrt(segmean(zk * zk) + EPS) * kg_ref[...]
    v = proj(OFF_VB, B_WIDTH)
    if prompt:
        vt = v.T
        kb_ref[...] = k.astype(BF16)
        qt_ref[...] = q.T
        kt_ref[...] = k.T
        vt_ref[...] = vt
        vtb_ref[...] = vt.astype(BF16)
        km_ref[0] = jnp.mean(k, axis=0, keepdims=True)
    else:
        k_ref[...] = k
        v_ref[...] = v
        q_ref[...] = q


def _mixer(x2d, prm, wst, bsp, prompt, seq=None):
    n = x2d.shape[0]
    assert n % ROW_TILE == 0
    nt = n // ROW_TILE
    full = lambda a: pl.BlockSpec(a.shape, lambda i: (0,) * a.ndim)
    rows = lambda w: pl.BlockSpec((ROW_TILE, w), lambda i: (i, 0))
    consts = [prm['g1'], prm['win'], prm['bgate'], prm['lng'], prm['lnb'], prm['seg'], wst, bsp,
              prm['qg'], prm['kg'], prm['wpa']]
    out_shape = [jax.ShapeDtypeStruct((n, D_MODEL), F32), jax.ShapeDtypeStruct((n, D_MODEL), F32)]
    out_specs = [rows(D_MODEL), rows(D_MODEL)]
    if prompt:
        tps = seq // ROW_TILE
        out_shape += [jax.ShapeDtypeStruct((n, B_WIDTH), BF16)]
        out_specs += [rows(B_WIDTH)]
        out_shape += [jax.ShapeDtypeStruct((n // seq, B_WIDTH, seq), dt) for dt in (F32, F32, F32, BF16)]
        out_specs += [pl.BlockSpec((None, B_WIDTH, ROW_TILE), lambda i: (i // tps, 0, i % tps))] * 4
        out_shape += [jax.ShapeDtypeStruct((nt, 1, B_WIDTH), F32)]
        out_specs += [pl.BlockSpec((1, 1, B_WIDTH), lambda i: (i, 0, 0))]
    else:
        out_shape += [jax.ShapeDtypeStruct((n, B_WIDTH), F32)] * 3
        out_specs += [rows(B_WIDTH)] * 3
        out_shape += [jax.ShapeDtypeStruct((n, A_WIDTH), F32)]
        out_specs += [rows(A_WIDTH)]
    return pl.pallas_call(
        functools.partial(_mixer_kernel, prompt),
        grid=(nt,),
        in_specs=[rows(D_MODEL)] + [full(a) for a in consts],
        out_specs=out_specs,
        out_shape=out_shape,
        compiler_params=pltpu.CompilerParams(dimension_semantics=("arbitrary",), vmem_limit_bytes=VMEM_LIMIT),
        name="mixer_p" if prompt else "mixer_s",
    )(x2d, *consts)


LOG2E = 1.4426950408889634


MOBA_HEADS_PER_STEP = 8


def _moba_prompt_kernel(n_sel, qt_ref, k_ref, vt_ref, km_ref, o_ref, bias_ref):
    i = pl.program_id(2)
    blk = MOBA_BLOCK
    nb = km_ref.shape[0]
    heads = range(MOBA_HEADS_PER_STEP)
    row = lax.broadcasted_iota(jnp.int32, (PAIR, blk), 0)
    brow = lax.broadcasted_iota(jnp.int32, (nb, blk), 0)
    lanes = [slice((hh // 2) * PAIR, (hh // 2 + 1) * PAIR) for hh in heads]
    hrows = [slice(hh * HEAD_DIM, (hh + 1) * HEAD_DIM) for hh in heads]
    qs = []
    for hh in heads:
        qm = jnp.where((row // HEAD_DIM) == hh % 2, qt_ref[lanes[hh], :], 0.0)
        gate = jnp.where(brow < i, _dot3(km_ref[:, lanes[hh]], qm), -jnp.inf)
        bias = jnp.full((nb, blk), NEG, F32)
        for r in range(n_sel):
            mx = jnp.max(gate, axis=0, keepdims=True)
            idx = jnp.min(jnp.where(gate == mx, brow, nb), axis=0, keepdims=True)
            hit = brow == idx
            bias = jnp.where(hit & (r < i), 0.0, bias)
            gate = jnp.where(hit, -jnp.inf, gate)
        bias_ref[hh] = bias
        qs.append((qm * (SCALE * LOG2E)).astype(BF16))

    own = pl.multiple_of(i * blk, blk)
    kpos = lax.broadcasted_iota(jnp.int32, (blk, blk), 0)
    qpos = lax.broadcasted_iota(jnp.int32, (blk, blk), 1)
    s_own = [_dot(k_ref[pl.ds(own, blk), lanes[hh]], qs[hh]) for hh in heads]
    state = []
    for hh in heads:
        s = jnp.where(kpos <= qpos, s_own[hh], NEG)
        m = jnp.max(s, axis=0, keepdims=True)
        p = jnp.exp2(s - m)
        l = jnp.sum(p, axis=0, keepdims=True)
        acc = _dot(vt_ref[hrows[hh], pl.ds(own, blk)], p.astype(BF16))
        state.append((m, l, acc))

    def body(c, carry):
        off = pl.multiple_of(c * (2 * blk), 2 * blk)
        s2 = [_dot(k_ref[pl.ds(off, 2 * blk), lanes[hh]], qs[hh]) for hh in heads]
        soft = []
        for hh in heads:
            m, l, acc = carry[hh]
            s_lo = s2[hh][:blk] + bias_ref[hh, pl.ds(2 * c, 1), :]
            s_hi = s2[hh][blk:] + bias_ref[hh, pl.ds(2 * c + 1, 1), :]
            m_new = jnp.maximum(m, jnp.maximum(jnp.max(s_lo, axis=0, keepdims=True),
                                               jnp.max(s_hi, axis=0, keepdims=True)))
            alpha = jnp.exp2(m - m_new)
            p_lo = jnp.exp2(s_lo - m_new)
            p_hi = jnp.exp2(s_hi - m_new)
            l = alpha * l + jnp.sum(p_lo, axis=0, keepdims=True) + jnp.sum(p_hi, axis=0, keepdims=True)
            soft.append((m_new, l, alpha, jnp.concatenate([p_lo, p_hi], axis=0).astype(BF16)))
        new = []
        for hh in heads:
            m_new, l, alpha, p = soft[hh]
            acc = alpha * carry[hh][2] + _dot(vt_ref[hrows[hh], pl.ds(off, 2 * blk)], p)
            new.append((m_new, l, acc))
        return tuple(new)

    state = lax.fori_loop(0, (i + 1) // 2, body, tuple(state))
    o_ref[...] = jnp.concatenate([acc / l for (_, l, acc) in state], axis=0).T


def _moba_prompt(qt, k, vt, kmean, bsz, seq):
    nb = seq // MOBA_BLOCK
    assert nb % 2 == 0
    n_sel = min(MOBA_TOPK, nb - 1)
    width = MOBA_HEADS_PER_STEP * HEAD_DIM
    return pl.pallas_call(
        functools.partial(_moba_prompt_kernel, n_sel),
        grid=(bsz, B_WIDTH // width, nb),
        in_specs=[
            pl.BlockSpec((None, width, MOBA_BLOCK), lambda b, p, i: (b, p, i)),
            pl.BlockSpec((seq, width), lambda b, p, i: (b, p)),
            pl.BlockSpec((None, width, seq), lambda b, p, i: (b, p, 0)),
            pl.BlockSpec((nb, width), lambda b, p, i: (b, p)),
        ],
        out_specs=pl.BlockSpec((MOBA_BLOCK, width), lambda b, p, i: (b * nb + i, p)),
        out_shape=jax.ShapeDtypeStruct((bsz * seq, B_WIDTH), F32),
        scratch_shapes=[pltpu.VMEM((MOBA_HEADS_PER_STEP, nb, MOBA_BLOCK), F32)],
        compiler_params=pltpu.CompilerParams(dimension_semantics=("arbitrary",) * 3, vmem_limit_bytes=VMEM_LIMIT),
        name="moba_p",
    )(qt, k, vt, kmean)


SAMPLE_BLOCKS_PER_STEP = 8
PAGES_PER_BLOCK = MOBA_BLOCK // PAGE_SIZE


def _diag_fold(o_all):
    rows = o_all.shape[0]
    n_new = rows // B_HEADS
    rh = lax.broadcasted_iota(jnp.int32, (rows, B_WIDTH), 0) // n_new
    lh = lax.broadcasted_iota(jnp.int32, (rows, B_WIDTH), 1) // HEAD_DIM
    o = jnp.where(rh == lh, o_all, 0.0)
    out = o[:, 0:LANES]
    for t in range(1, B_WIDTH // LANES):
        out = out + o[:, t * LANES:(t + 1) * LANES]
    return out


def _moba_sample_kernel(n_new, nb_past, n_sel, pt_ref, q_ref, kn_ref, vn_ref, ck_hbm, cv_hbm, o_ref,
                        g_scr, m_scr, l_scr, o_scr, kbuf, vbuf, sems):
    npg = SAMPLE_BLOCKS_PER_STEP * PAGES_PER_BLOCK
    seq_id = pl.program_id(0)
    step = pl.program_id(1)
    nsteps = pl.num_programs(1)
    rows = B_HEADS * n_new

    lin = seq_id * nsteps + step
    cur = lin % 2

    def page_copies(lin_step, slot):
        b, s = lin_step // nsteps, lin_step % nsteps
        copies = []
        for j in range(npg):
            page = pt_ref[b, s * npg + j]
            copies.append(pltpu.make_async_copy(ck_hbm.at[page], kbuf.at[slot, j], sems.at[slot]))
            copies.append(pltpu.make_async_copy(cv_hbm.at[page], vbuf.at[slot, j], sems.at[slot]))
        return copies

    @pl.when(lin == 0)
    def _():
        for cp in page_copies(lin, cur):
            cp.start()

    @pl.when(lin + 1 < pl.num_programs(0) * nsteps)
    def _():
        for cp in page_copies(lin + 1, 1 - cur):
            cp.start()

    for cp in page_copies(lin, cur):
        cp.wait()
    k_refs = [kbuf.at[cur, j] for j in range(npg)]
    v_refs = [vbuf.at[cur, j] for j in range(npg)]
    rh = lax.broadcasted_iota(jnp.int32, (rows, B_WIDTH), 0) // n_new
    lh = lax.broadcasted_iota(jnp.int32, (rows, B_WIDTH), 1) // HEAD_DIM
    q = q_ref[...]
    qbd = jnp.where(rh == lh, jnp.concatenate([q] * B_HEADS, axis=0), 0.0).astype(BF16)
    lane = lax.broadcasted_iota(jnp.int32, (rows, LANES), 1)

    @pl.when(step == 0)
    def _():
        g_scr[...] = jnp.zeros_like(g_scr)
        m_scr[...] = jnp.zeros_like(m_scr)
        l_scr[...] = jnp.zeros_like(l_scr)

    s_pages = [_dot(qbd, k_refs[j][...].astype(BF16)) for j in range(npg)]
    probs, gates, maxes, sums = [], [], [], []
    for bi in range(SAMPLE_BLOCKS_PER_STEP):
        blk_pages = s_pages[bi * PAGES_PER_BLOCK:(bi + 1) * PAGES_PER_BLOCK]
        raw = blk_pages[0]
        mx = blk_pages[0]
        for sp in blk_pages[1:]:
            raw = raw + sp
            mx = jnp.maximum(mx, sp)
        gates.append(jnp.sum(raw, axis=1, keepdims=True) * (1.0 / MOBA_BLOCK))
        m_n = jnp.max(mx, axis=1, keepdims=True) * SCALE
        l_n = jnp.zeros((rows, 1), F32)
        for sp in blk_pages:
            p = jnp.exp(sp * SCALE - m_n)
            l_n = l_n + jnp.sum(p, axis=1, keepdims=True)
            probs.append(p.astype(BF16))
        maxes.append(m_n)
        sums.append(l_n)
    outs = []
    for bi in range(SAMPLE_BLOCKS_PER_STEP):
        o_all = _dot_t(probs[bi * PAGES_PER_BLOCK], v_refs[bi * PAGES_PER_BLOCK][...].astype(BF16))
        for pg in range(1, PAGES_PER_BLOCK):
            j = bi * PAGES_PER_BLOCK + pg
            o_all = o_all + _dot_t(probs[j], v_refs[j][...].astype(BF16))
        outs.append(o_all)
    g_new, m_new, l_new = g_scr[0:rows], m_scr[0:rows], l_scr[0:rows]
    for bi in range(SAMPLE_BLOCKS_PER_STEP):
        n = step * SAMPLE_BLOCKS_PER_STEP + bi
        o_scr[n] = _diag_fold(outs[bi])
        g_new = jnp.where(lane == n_new + n, gates[bi], g_new)
        m_new = jnp.where(lane == n_new + n, maxes[bi], m_new)
        l_new = jnp.where(lane == n_new + n, sums[bi], l_new)
    g_scr[0:rows] = g_new
    m_scr[0:rows] = m_new
    l_scr[0:rows] = l_new

    @pl.when(step == pl.num_programs(1) - 1)
    def _():
        ent = lax.broadcasted_iota(jnp.int32, (LANES, LANES), 0)
        col = lax.broadcasted_iota(jnp.int32, (LANES, LANES), 1)
        is_own = ent < n_new
        is_blk = (ent >= n_new) & (ent < n_new + nb_past)
        gate = jnp.where(is_blk, g_scr[...].T, -jnp.inf)
        sel = jnp.zeros((LANES, LANES), jnp.bool_)
        for _ in range(n_sel):
            mx = jnp.max(gate, axis=0, keepdims=True)
            idx = jnp.min(jnp.where(gate == mx, ent, LANES), axis=0, keepdims=True)
            hit = ent == idx
            sel = sel | hit
            gate = jnp.where(hit, -jnp.inf, gate)
        qpad = jnp.concatenate([qbd, jnp.zeros((LANES - rows, B_WIDTH), BF16)], axis=0)
        s_own = _dot_t(kn_ref[...].astype(BF16), qpad) * SCALE
        st = jnp.concatenate([s_own, m_scr[...].T[n_new:]], axis=0)
        valid = (is_own & (ent <= col % n_new)) | (is_blk & sel)
        st = jnp.where(valid, st, NEG)
        m = jnp.max(st, axis=0, keepdims=True)
        w = jnp.where(valid, jnp.exp(st - m), 0.0)
        den = jnp.sum(w * jnp.where(is_own, 1.0, l_scr[...].T), axis=0, keepdims=True)
        wn = (w / den).T[0:rows]
        num = _diag_fold(_dot(wn[:, 0:n_new].astype(BF16), vn_ref[...].astype(BF16)))
        for nblk in range(nb_past):
            num = num + wn[:, n_new + nblk:n_new + nblk + 1] * o_scr[nblk]
        out = jnp.where(rh == lh, jnp.concatenate([num] * (B_WIDTH // LANES), axis=1), 0.0)
        o_ref[...] = jnp.sum(out.reshape(B_HEADS, n_new, B_WIDTH), axis=0)


def _moba_sample(q, k_new, v_new, cache_kt, cache_vt, page_table, past_len):
    db, n_new, _ = q.shape
    assert past_len % MOBA_BLOCK == 0
    nb_past = past_len // MOBA_BLOCK
    assert nb_past % SAMPLE_BLOCKS_PER_STEP == 0 and n_new % 8 == 0
    assert n_new + nb_past <= LANES and B_HEADS * n_new <= LANES
    n_sel = min(MOBA_TOPK, nb_past)
    nsteps = nb_past // SAMPLE_BLOCKS_PER_STEP
    npg = SAMPLE_BLOCKS_PER_STEP * PAGES_PER_BLOCK
    rows = B_HEADS * n_new

    tok_spec = pl.BlockSpec((None, n_new, B_WIDTH), lambda b, s, pt: (b, 0, 0))
    any_spec = pl.BlockSpec(memory_space=pl.ANY)
    page_buf = pltpu.VMEM((2, npg, B_WIDTH, PAGE_SIZE), F32)
    grid_spec = pltpu.PrefetchScalarGridSpec(
        num_scalar_prefetch=1,
        grid=(db, nsteps),
        in_specs=[tok_spec] * 3 + [any_spec] * 2,
        out_specs=tok_spec,
        scratch_shapes=[pltpu.VMEM((LANES, LANES), F32)] * 3 + [pltpu.VMEM((nb_past, rows, LANES), F32)]
        + [page_buf, page_buf, pltpu.SemaphoreType.DMA((2,))],
    )
    return pl.pallas_call(
        functools.partial(_moba_sample_kernel, n_new, nb_past, n_sel),
        grid_spec=grid_spec,
        out_shape=jax.ShapeDtypeStruct((db, n_new, B_WIDTH), F32),
        compiler_params=pltpu.CompilerParams(dimension_semantics=("arbitrary",) * 2, vmem_limit_bytes=VMEM_LIMIT),
        name="moba_s",
    )(page_table, q, k_new, v_new, cache_kt, cache_vt)


def _merge_kernel(x_ref, pa_ref, gb_ref, yb_ref, wpb_ref, wo_ref, g2_ref, wr_ref, br_ref, h_ref, xn_ref, route_ref):
    mix = pa_ref[...] + gb_ref[...] * _dot(yb_ref[...].astype(BF16), wpb_ref[...])
    h = x_ref[...] + _dot(mix.astype(BF16), wo_ref[...])
    h_ref[...] = h
    ms = jnp.mean(h * h, axis=-1, keepdims=True)
    xn = (h * lax.rsqrt(ms + EPS)) * g2_ref[...]
    xn_ref[...] = xn
    logits = _dot3(xn, wr_ref[...]) + br_ref[...]
    lane = lax.broadcasted_iota(jnp.int32, logits.shape, 1)
    is_grp = (lane >= N_EXPERTS) & (lane < N_EXPERTS + MOE_GROUPS)
    gl = jnp.where(is_grp, logits, -jnp.inf)
    gmax = jnp.max(gl, axis=1, keepdims=True)
    grp = jnp.min(jnp.where(gl == gmax, lane, LANES), axis=1, keepdims=True) - N_EXPERTS
    p_grp = 1.0 / jnp.sum(jnp.exp(gl - gmax), axis=1, keepdims=True)
    el = jnp.where((lane // MOE_PER_GROUP == grp) & (lane < N_EXPERTS), logits, -jnp.inf)
    v1 = jnp.max(el, axis=1, keepdims=True)
    i1 = jnp.min(jnp.where(el == v1, lane, LANES), axis=1, keepdims=True)
    el2 = jnp.where(lane == i1, -jnp.inf, el)
    v2 = jnp.max(el2, axis=1, keepdims=True)
    i2 = jnp.min(jnp.where(el2 == v2, lane, LANES), axis=1, keepdims=True)
    e2 = jnp.exp(v2 - v1)
    w1 = p_grp / (1.0 + e2)
    w2 = p_grp * e2 / (1.0 + e2)
    route = jnp.where(lane == 0, i1.astype(F32), jnp.where(lane == 1, i2.astype(F32), 0.0))
    route_ref[...] = jnp.where(lane == 2, w1, jnp.where(lane == 3, w2, route))


def _merge(x2d, pa, gb, yb, prm):
    n = x2d.shape[0]
    nt = n // ROW_TILE
    full = lambda a: pl.BlockSpec(a.shape, lambda i: (0,) * a.ndim)
    rows = lambda w: pl.BlockSpec((ROW_TILE, w), lambda i: (i, 0))
    consts = [prm['wpb'], prm['wo'], prm['g2'], prm['wr'], prm['br']]
    return pl.pallas_call(
        _merge_kernel,
        grid=(nt,),
        in_specs=[rows(D_MODEL), rows(D_MODEL), rows(D_MODEL), rows(B_WIDTH)] + [full(a) for a in consts],
        out_specs=[rows(D_MODEL), rows(D_MODEL), rows(LANES)],
        out_shape=[jax.ShapeDtypeStruct((n, D_MODEL), F32), jax.ShapeDtypeStruct((n, D_MODEL), F32),
                   jax.ShapeDtypeStruct((n, LANES), F32)],
        compiler_params=pltpu.CompilerParams(dimension_semantics=("arbitrary",), vmem_limit_bytes=VMEM_LIMIT),
        name="merge",
    )(x2d, pa, gb, yb, *consts)


EXPERT_TILE = 512
POS_ROWS = 8
DMA_UNROLL = 8


def _rank_kernel(route_ref, pos_ref, cnt_ref, carry_ref):
    @pl.when(pl.program_id(0) == 0)
    def _():
        carry_ref[...] = jnp.zeros_like(carry_ref)

    route = route_ref[...]
    tm = route.shape[0]
    lane = lax.broadcasted_iota(jnp.int32, (tm, LANES), 1)
    lanef = lane.astype(F32)
    i1, i2 = route[:, 0:1], route[:, 1:2]
    hit1, hit2 = lanef == i1, lanef == i2
    onehot = jnp.where(hit1 | hit2, 1.0, 0.0)
    earlier = lax.broadcasted_iota(jnp.int32, (tm, tm), 1) < lax.broadcasted_iota(jnp.int32, (tm, tm), 0)
    before = _dot(jnp.where(earlier, 1.0, 0.0).astype(BF16), onehot.astype(BF16)) + carry_ref[...]
    r1 = jnp.sum(jnp.where(hit1, before, 0.0), axis=1, keepdims=True)
    r2 = jnp.sum(jnp.where(hit2, before, 0.0), axis=1, keepdims=True)
    carry_ref[...] = carry_ref[...] + jnp.sum(onehot, axis=0, keepdims=True)
    packed = jnp.where(lane == 0, i1, jnp.where(lane == 1, r1, jnp.where(lane == 2, i2, jnp.where(lane == 3, r2, 0.0))))
    pos_ref[...] = packed.T[0:POS_ROWS].astype(jnp.int32)
    cnt_ref[...] = carry_ref[...].astype(jnp.int32)


def _dispatch_kernel(off_ref, cnt_ref, pos_ref, xn_ref, xs_hbm, zero_ref, zsem, sem):
    i = pl.program_id(0)
    tm = pos_ref.shape[1]

    last = N_EXPERTS - 1
    tg = zero_ref.shape[0]
    n_act = off_ref[last] // tg + (cnt_ref[last] + tg - 1) // tg
    kmax = xs_hbm.shape[0] // tg
    fills = [(cnt_ref[e] % tg != 0, off_ref[e] + (cnt_ref[e] // tg) * tg) for e in range(N_EXPERTS)]
    fills += [(n_act + j < kmax, (n_act + j) * tg) for j in range(N_EXPERTS)]

    def zero_copy(first):
        return pltpu.make_async_copy(zero_ref, xs_hbm.at[pl.ds(pl.multiple_of(first, tg), tg)], zsem)

    @pl.when(i == 0)
    def _():
        zero_ref[...] = jnp.zeros_like(zero_ref)
        for needed, first in fills:
            @pl.when(needed)
            def _():
                zero_copy(first).start()
        for needed, first in fills:
            @pl.when(needed)
            def _():
                zero_copy(first).wait()

    def row_copies(r):
        src = xn_ref.at[pl.ds(r, 1)]
        return [pltpu.make_async_copy(src, xs_hbm.at[pl.ds(pos_ref[s, r], 1)], sem)
                for s in range(2)]

    def issue(r, c):
        for cp in row_copies(r):
            cp.start()
        return c

    def drain(r, c):
        for cp in row_copies(r):
            cp.wait()
        return c

    lax.fori_loop(0, tm, issue, 0, unroll=DMA_UNROLL)
    lax.fori_loop(0, tm, drain, 0, unroll=DMA_UNROLL)


def _ffn_kernel(te_ref, nact_ref, xs_ref, w1_ref, w3_ref, w2_ref, ys_ref):
    active = pl.program_id(0) < nact_ref[0]

    @pl.when(active)
    def _():
        x = xs_ref[...].astype(BF16)
        hid = jax.nn.silu(_dot(x, w1_ref[...].astype(BF16))) * _dot(x, w3_ref[...].astype(BF16))
        ys_ref[...] = _dot(hid.astype(BF16), w2_ref[...].astype(BF16))

    @pl.when(jnp.logical_not(active))
    def _():
        ys_ref[...] = jnp.zeros_like(ys_ref)


def _combine_kernel(pos_ref, h_ref, route_ref, ys_hbm, o_ref, g1_ref, g2_ref, sem):
    tm = pos_ref.shape[1]
    bufs = (g1_ref, g2_ref)

    def row_copies(r):
        return [pltpu.make_async_copy(ys_hbm.at[pl.ds(pos_ref[s, r], 1)],
                                      bufs[s].at[pl.ds(r, 1)], sem) for s in range(2)]

    def issue(r, c):
        for cp in row_copies(r):
            cp.start()
        return c

    def drain(r, c):
        for cp in row_copies(r):
            cp.wait()
        return c

    lax.fori_loop(0, tm, issue, 0, unroll=DMA_UNROLL)
    lax.fori_loop(0, tm, drain, 0, unroll=DMA_UNROLL)
    route = route_ref[...]
    o_ref[...] = h_ref[...] + route[:, 2:3] * g1_ref[...] + route[:, 3:4] * g2_ref[...]


def _moe(xn, h, route, w1, w3, w2):
    n = xn.shape[0]
    tm = ROW_TILE
    nt = n // tm
    tg = min(EXPERT_TILE, max(ROW_TILE, n // N_EXPERTS))
    kmax = 2 * n // tg + N_EXPERTS
    cparams = pltpu.CompilerParams(dimension_semantics=("arbitrary",), vmem_limit_bytes=VMEM_LIMIT)

    pos, cnt = pl.pallas_call(
        _rank_kernel,
        grid=(nt,),
        in_specs=[pl.BlockSpec((tm, LANES), lambda i: (i, 0))],
        out_specs=[pl.BlockSpec((None, POS_ROWS, tm), lambda i: (i, 0, 0)), pl.BlockSpec((1, LANES), lambda i: (0, 0))],
        out_shape=[jax.ShapeDtypeStruct((nt, POS_ROWS, tm), jnp.int32), jax.ShapeDtypeStruct((1, LANES), jnp.int32)],
        scratch_shapes=[pltpu.VMEM((1, LANES), F32)],
        compiler_params=cparams, name="moe_rank",
    )(route)

    cnt = cnt[0, :N_EXPERTS]
    ntile = (cnt + tg - 1) // tg
    cum = jnp.cumsum(ntile)
    off = ((cum - ntile) * tg).astype(jnp.int32)
    nact = cum[-1:].astype(jnp.int32)
    tile = jnp.minimum(jnp.arange(kmax, dtype=jnp.int32), nact[0] - 1)
    tile_e = jnp.sum(tile[:, None] >= cum[None, :], axis=1).astype(jnp.int32)
    seg_off = lambda e: jnp.sum(jnp.where(e[..., None] == jnp.arange(N_EXPERTS), off, 0), axis=-1)
    dest = jnp.concatenate([seg_off(pos[:, 0:1]) + pos[:, 1:2], seg_off(pos[:, 2:3]) + pos[:, 3:4], pos[:, 2:]], axis=1)

    pos_spec = pl.BlockSpec((None, POS_ROWS, tm), lambda i, *_: (i, 0, 0), memory_space=pltpu.SMEM)
    any_spec = pl.BlockSpec(memory_space=pl.ANY)
    xs = pl.pallas_call(
        _dispatch_kernel,
        grid_spec=pltpu.PrefetchScalarGridSpec(
            num_scalar_prefetch=2, grid=(nt,),
            in_specs=[pos_spec, pl.BlockSpec((tm, D_MODEL), lambda i, *_: (i, 0))], out_specs=any_spec,
            scratch_shapes=[pltpu.VMEM((tg, D_MODEL), F32), pltpu.SemaphoreType.DMA, pltpu.SemaphoreType.DMA]),
        out_shape=jax.ShapeDtypeStruct((kmax * tg, D_MODEL), F32),
        compiler_params=cparams, name="moe_dispatch",
    )(off, cnt, dest, xn)

    act = lambda k, te, na: (jnp.minimum(k, na[0] - 1), 0)
    ys = pl.pallas_call(
        _ffn_kernel,
        grid_spec=pltpu.PrefetchScalarGridSpec(
            num_scalar_prefetch=2, grid=(kmax,),
            in_specs=[pl.BlockSpec((tg, D_MODEL), act),
                      pl.BlockSpec((None, D_MODEL, D_EXPERT), lambda k, te, na: (te[k], 0, 0)),
                      pl.BlockSpec((None, D_MODEL, D_EXPERT), lambda k, te, na: (te[k], 0, 0)),
                      pl.BlockSpec((None, D_EXPERT, D_MODEL), lambda k, te, na: (te[k], 0, 0))],
            out_specs=pl.BlockSpec((tg, D_MODEL), lambda k, te, na: (k, 0))),
        out_shape=jax.ShapeDtypeStruct((kmax * tg, D_MODEL), F32),
        compiler_params=cparams, name="moe_ffn",
    )(tile_e, nact, xs, w1, w3, w2)

    rows = lambda w: pl.BlockSpec((tm, w), lambda i, *_: (i, 0))
    return pl.pallas_call(
        _combine_kernel,
        grid_spec=pltpu.PrefetchScalarGridSpec(
            num_scalar_prefetch=0, grid=(nt,),
            in_specs=[pos_spec, rows(D_MODEL), rows(LANES), any_spec],
            out_specs=rows(D_MODEL),
            scratch_shapes=[pltpu.VMEM((tm, D_MODEL), F32), pltpu.VMEM((tm, D_MODEL), F32), pltpu.SemaphoreType.DMA]),
        out_shape=jax.ShapeDtypeStruct((n, D_MODEL), F32),
        compiler_params=cparams, name="moe_combine",
    )(dest, h, route, ys)


def _prepare(norm1_g, w_in, b_gate, ln_v_g, ln_v_b, qn_g, kn_g, w_pa, w_pb, w_o, norm2_g,
             w_rg, b_rg, w_re, b_re):
    seg = jnp.kron(jnp.eye(B_HEADS, dtype=F32), jnp.full((HEAD_DIM, HEAD_DIM), 1.0 / HEAD_DIM, F32))
    pad = LANES - N_EXPERTS - MOE_GROUPS
    return dict(
        g1=norm1_g.reshape(1, D_MODEL), win=w_in.astype(BF16), bgate=b_gate.reshape(1, 2 * D_MODEL),
        lng=ln_v_g.reshape(1, A_WIDTH), lnb=ln_v_b.reshape(1, A_WIDTH), seg=seg.astype(BF16),
        qg=jnp.tile(qn_g, B_HEADS).reshape(1, B_WIDTH), kg=jnp.tile(kn_g, B_HEADS).reshape(1, B_WIDTH),
        wpa=w_pa.astype(BF16), wpb=w_pb.astype(BF16), wo=w_o.astype(BF16), g2=norm2_g.reshape(1, D_MODEL),
        wr=jnp.pad(jnp.concatenate([w_re, w_rg], axis=1), ((0, 0), (0, pad))),
        br=jnp.pad(jnp.concatenate([b_re, b_rg]), (0, pad)).reshape(1, LANES),
    )


def _spatial_weights(w_sp, b_sp, length):
    reps = CHUNK // length
    tril = jnp.tril(jnp.ones((length, length), F32))
    w = w_sp[:, :length, :length] * tril
    w = jax.vmap(lambda m: jnp.kron(jnp.eye(reps, dtype=F32), m))(w)
    bias = jnp.tile(b_sp[:, :length].T, (reps, 1))
    side_by_side = jnp.transpose(w, (1, 0, 2)).reshape(CHUNK, A_GROUPS * CHUNK)
    return side_by_side.astype(BF16), jnp.repeat(bias, A_GROUP_DIM, axis=1)


def kernel(x_prompt, x_sample, cache_k, cache_v, page_table, norm1_g, w_in, b_gate, ln_v_g, ln_v_b, w_sp, b_sp,
           qn_g, kn_g, w_pa, w_pb, w_o, norm2_g, w_rg, b_rg, w_re, b_re, w1, w3, w2):
    bsz, seq, _ = x_prompt.shape
    db, n_new, _ = x_sample.shape
    past_len = page_table.shape[1] * PAGE_SIZE
    prm = _prepare(norm1_g, w_in, b_gate, ln_v_g, ln_v_b, qn_g, kn_g, w_pa, w_pb, w_o, norm2_g,
                   w_rg, b_rg, w_re, b_re)

    xp = x_prompt.reshape(bsz * seq, D_MODEL)
    wst, bsp = _spatial_weights(w_sp, b_sp, CHUNK)
    pa, gb, k_b, qt, kt, vt, vt_b, kmean = _mixer(xp, prm, wst, bsp, True, seq)
    yb = _moba_prompt(qt, k_b, vt_b, kmean.reshape(-1, B_WIDTH), bsz, seq)
    h, xn, route = _merge(xp, pa, gb, yb, prm)
    y_p = _moe(xn, h, route, w1, w3, w2)

    xs = x_sample.reshape(db * n_new, D_MODEL)
    wst, bsp = _spatial_weights(w_sp, b_sp, n_new)
    pa, gb, q_s, k_s, v_s, va_s = _mixer(xs, prm, wst, bsp, False)
    n_phys = cache_k.shape[0]
    ckt = jnp.transpose(cache_k, (0, 2, 3, 1)).reshape(n_phys, B_WIDTH, PAGE_SIZE)
    cvt = jnp.transpose(cache_v, (0, 2, 3, 1)).reshape(n_phys, B_WIDTH, PAGE_SIZE)
    tok3 = lambda a: a.reshape(db, n_new, B_WIDTH)
    yb = _moba_sample(tok3(q_s), tok3(k_s), tok3(v_s), ckt, cvt, page_table, past_len)
    h, xn, route = _merge(xs, pa, gb, yb.reshape(db * n_new, B_WIDTH), prm)
    y_s = _moe(xn, h, route, w1, w3, w2)

    heads = lambda a, lead: a.reshape(*lead, B_HEADS, HEAD_DIM)
    from_t = lambda a: jnp.transpose(a.reshape(bsz, B_HEADS, HEAD_DIM, seq), (0, 3, 1, 2))
    return (y_p.reshape(bsz, seq, D_MODEL), y_s.reshape(db, n_new, D_MODEL),
            from_t(kt), from_t(vt),
            heads(k_s, (db, n_new)), heads(v_s, (db, n_new)),
            va_s.reshape(db, n_new, A_WIDTH))
```

```python
import functools

import jax
import jax.numpy as jnp
from jax import lax
from jax.experimental import pallas as pl
from jax.experimental.pallas import tpu as pltpu

F32 = jnp.float32
BF16 = jnp.bfloat16

D_MODEL = 1024
A_GROUPS = 8
A_GROUP_DIM = 64
A_WIDTH = A_GROUPS * A_GROUP_DIM
CHUNK = 128
B_HEADS = 8
HEAD_DIM = 64
B_WIDTH = B_HEADS * HEAD_DIM
MOBA_BLOCK = 256
MOBA_TOPK = 3
PAGE_SIZE = 128
SCALE = HEAD_DIM ** -0.5
MOE_GROUPS = 4
MOE_PER_GROUP = 4
N_EXPERTS = MOE_GROUPS * MOE_PER_GROUP
D_EXPERT = 512
EPS = 1e-6
OFF_U = 0
OFF_VA = OFF_U + A_WIDTH
OFF_Q = OFF_VA + A_WIDTH
OFF_K = OFF_Q + B_WIDTH
OFF_VB = OFF_K + B_WIDTH
OFF_GATE = OFF_VB + B_WIDTH

LANES = 128
MXU_COLS = 256
ROW_TILE = 256
PAIR = 2 * HEAD_DIM
VMEM_LIMIT = 56 * 1024 * 1024
NEG = -1e30


def _dot(a, b):
    return jnp.dot(a, b, preferred_element_type=F32)


def _dot_t(a, b):
    return lax.dot_general(a, b, (((1,), (1,)), ((), ())), preferred_element_type=F32)


def _split(a):
    hi = a.astype(BF16)
    lo = (a - hi.astype(F32)).astype(BF16)
    return hi, lo


def _dot3(a, b):
    ah, al = _split(a)
    bh, bl = _split(b)
    return _dot(ah, bh) + _dot(ah, bl) + _dot(al, bh)


def _mixer_kernel(prompt, x_ref, g1_ref, win_ref, bgate_ref, lng_ref, lnb_ref, seg_ref, wst_ref, bsp_ref,
                  qg_ref, kg_ref, wpa_ref, *outs):
    if prompt:
        pa_ref, gb_ref, kb_ref, qt_ref, kt_ref, vt_ref, vtb_ref, km_ref = outs
    else:
        pa_ref, gb_ref, q_ref, k_ref, v_ref, va_ref = outs
    x = x_ref[...]
    ms = jnp.mean(x * x, axis=-1, keepdims=True)
    xb = ((x * lax.rsqrt(ms + EPS)) * g1_ref[...]).astype(BF16)
    seg = seg_ref[...]

    def proj(off, width):
        return _dot(xb, win_ref[:, off:off + width])

    def segmean(a):
        return _dot(a.astype(BF16), seg)

    u = jax.nn.gelu(proj(OFF_U, A_WIDTH))
    va = jax.nn.gelu(proj(OFF_VA, A_WIDTH))
    xc = va - segmean(va)
    va = xc * lax.rsqrt(segmean(xc * xc) + EPS) * lng_ref[...] + lnb_ref[...]
    if not prompt:
        va_ref[...] = va

    gpt = MXU_COLS // A_GROUP_DIM
    grp = lax.broadcasted_iota(jnp.int32, (CHUNK, MXU_COLS), 1) // A_GROUP_DIM
    vb = va.astype(BF16)
    ya = []
    for c in range(ROW_TILE // CHUNK):
        rows = slice(c * CHUNK, (c + 1) * CHUNK)
        mixed = []
        for t in range(A_WIDTH // MXU_COLS):
            vt = vb[rows, t * MXU_COLS:(t + 1) * MXU_COLS]
            stack = jnp.concatenate([jnp.where(grp == g, vt, jnp.zeros_like(vt)) for g in range(gpt)], axis=0)
            mixed.append(_dot(wst_ref[:, t * gpt * CHUNK:(t + 1) * gpt * CHUNK], stack))
        ya.append(u[rows] * (jnp.concatenate(mixed, axis=1) + bsp_ref[...]))
    ya = jnp.concatenate(ya, axis=0)

    gates_a = jax.nn.sigmoid(proj(OFF_GATE, D_MODEL) + bgate_ref[:, :D_MODEL])
    pa_ref[...] = gates_a * _dot(ya.astype(BF16), wpa_ref[...])
    gb_ref[...] = jax.nn.sigmoid(proj(OFF_GATE + D_MODEL, D_MODEL) + bgate_ref[:, D_MODEL:])

    zq = proj(OFF_Q, B_WIDTH)
    q = zq * lax.rsqrt(segmean(zq * zq) + EPS) * qg_ref[...]
    zk = proj(OFF_K, B_WIDTH)
    k = zk * lax.rsqrt(segmean(zk * zk) + EPS) * kg_ref[...]
    v = proj(OFF_VB, B_WIDTH)
    if prompt:
        vt = v.T
        kb_ref[...] = k.astype(BF16)
        qt_ref[...] = q.T
        kt_ref[...] = k.T
        vt_ref[...] = vt
        vtb_ref[...] = vt.astype(BF16)
        km_ref[0] = jnp.mean(k, axis=0, keepdims=True)
    else:
        k_ref[...] = k
        v_ref[...] = v
        q_ref[...] = q


def _mixer(x2d, prm, wst, bsp, prompt, seq=None):
    n = x2d.shape[0]
    assert n % ROW_TILE == 0
    nt = n // ROW_TILE
    full = lambda a: pl.BlockSpec(a.shape, lambda i: (0,) * a.ndim)
    rows = lambda w: pl.BlockSpec((ROW_TILE, w), lambda i: (i, 0))
    consts = [prm['g1'], prm['win'], prm['bgate'], prm['lng'], prm['lnb'], prm['seg'], wst, bsp,
              prm['qg'], prm['kg'], prm['wpa']]
    out_shape = [jax.ShapeDtypeStruct((n, D_MODEL), F32), jax.ShapeDtypeStruct((n, D_MODEL), F32)]
    out_specs = [rows(D_MODEL), rows(D_MODEL)]
    if prompt:
        tps = seq // ROW_TILE
        out_shape += [jax.ShapeDtypeStruct((n, B_WIDTH), BF16)]
        out_specs += [rows(B_WIDTH)]
        out_shape += [jax.ShapeDtypeStruct((n // seq, B_WIDTH, seq), dt) for dt in (F32, F32, F32, BF16)]
        out_specs += [pl.BlockSpec((None, B_WIDTH, ROW_TILE), lambda i: (i // tps, 0, i % tps))] * 4
        out_shape += [jax.ShapeDtypeStruct((nt, 1, B_WIDTH), F32)]
        out_specs += [pl.BlockSpec((1, 1, B_WIDTH), lambda i: (i, 0, 0))]
    else:
        out_shape += [jax.ShapeDtypeStruct((n, B_WIDTH), F32)] * 3
        out_specs += [rows(B_WIDTH)] * 3
        out_shape += [jax.ShapeDtypeStruct((n, A_WIDTH), F32)]
        out_specs += [rows(A_WIDTH)]
    return pl.pallas_call(
        functools.partial(_mixer_kernel, prompt),
        grid=(nt,),
        in_specs=[rows(D_MODEL)] + [full(a) for a in consts],
        out_specs=out_specs,
        out_shape=out_shape,
        compiler_params=pltpu.CompilerParams(dimension_semantics=("arbitrary",), vmem_limit_bytes=VMEM_LIMIT),
        name="mixer_p" if prompt else "mixer_s",
    )(x2d, *consts)


LOG2E = 1.4426950408889634


MOBA_HEADS_PER_STEP = 8


def _moba_prompt_kernel(n_sel, qt_ref, k_ref, vt_ref, km_ref, o_ref, bias_ref):
    i = pl.program_id(2)
    blk = MOBA_BLOCK
    nb = km_ref.shape[0]
    heads = range(MOBA_HEADS_PER_STEP)
    row = lax.broadcasted_iota(jnp.int32, (PAIR, blk), 0)
    brow = lax.broadcasted_iota(jnp.int32, (nb, blk), 0)
    lanes = [slice((hh // 2) * PAIR, (hh // 2 + 1) * PAIR) for hh in heads]
    hrows = [slice(hh * HEAD_DIM, (hh + 1) * HEAD_DIM) for hh in heads]
    qs = []
    for hh in heads:
        qm = jnp.where((row // HEAD_DIM) == hh % 2, qt_ref[lanes[hh], :], 0.0)
        gate = jnp.where(brow < i, _dot3(km_ref[:, lanes[hh]], qm), -jnp.inf)
        bias = jnp.full((nb, blk), NEG, F32)
        for r in range(n_sel):
            mx = jnp.max(gate, axis=0, keepdims=True)
            idx = jnp.min(jnp.where(gate == mx, brow, nb), axis=0, keepdims=True)
            hit = brow == idx
            bias = jnp.where(hit & (r < i), 0.0, bias)
            gate = jnp.where(hit, -jnp.inf, gate)
        bias_ref[hh] = bias
        qs.append((qm * (SCALE * LOG2E)).astype(BF16))

    own = pl.multiple_of(i * blk, blk)
    kpos = lax.broadcasted_iota(jnp.int32, (blk, blk), 0)
    qpos = lax.broadcasted_iota(jnp.int32, (blk, blk), 1)
    s_own = [_dot(k_ref[pl.ds(own, blk), lanes[hh]], qs[hh]) for hh in heads]
    state = []
    for hh in heads:
        s = jnp.where(kpos <= qpos, s_own[hh], NEG)
        m = jnp.max(s, axis=0, keepdims=True)
        p = jnp.exp2(s - m)
        l = jnp.sum(p, axis=0, keepdims=True)
        acc = _dot(vt_ref[hrows[hh], pl.ds(own, blk)], p.astype(BF16))
        state.append((m, l, acc))

    def body(c, carry):
        off = pl.multiple_of(c * (2 * blk), 2 * blk)
        s2 = [_dot(k_ref[pl.ds(off, 2 * blk), lanes[hh]], qs[hh]) for hh in heads]
        soft = []
        for hh in heads:
            m, l, acc = carry[hh]
            s_lo = s2[hh][:blk] + bias_ref[hh, pl.ds(2 * c, 1), :]
            s_hi = s2[hh][blk:] + bias_ref[hh, pl.ds(2 * c + 1, 1), :]
            m_new = jnp.maximum(m, jnp.maximum(jnp.max(s_lo, axis=0, keepdims=True),
                                               jnp.max(s_hi, axis=0, keepdims=True)))
            alpha = jnp.exp2(m - m_new)
            p_lo = jnp.exp2(s_lo - m_new)
            p_hi = jnp.exp2(s_hi - m_new)
            l = alpha * l + jnp.sum(p_lo, axis=0, keepdims=True) + jnp.sum(p_hi, axis=0, keepdims=True)
            soft.append((m_new, l, alpha, jnp.concatenate([p_lo, p_hi], axis=0).astype(BF16)))
        new = []
        for hh in heads:
            m_new, l, alpha, p = soft[hh]
            acc = alpha * carry[hh][2] + _dot(vt_ref[hrows[hh], pl.ds(off, 2 * blk)], p)
            new.append((m_new, l, acc))
        return tuple(new)

    state = lax.fori_loop(0, (i + 1) // 2, body, tuple(state))
    o_ref[...] = jnp.concatenate([acc / l for (_, l, acc) in state], axis=0).T


def _moba_prompt(qt, k, vt, kmean, bsz, seq):
    nb = seq // MOBA_BLOCK
    assert nb % 2 == 0
    n_sel = min(MOBA_TOPK, nb - 1)
    width = MOBA_HEADS_PER_STEP * HEAD_DIM
    return pl.pallas_call(
        functools.partial(_moba_prompt_kernel, n_sel),
        grid=(bsz, B_WIDTH // width, nb),
        in_specs=[
            pl.BlockSpec((None, width, MOBA_BLOCK), lambda b, p, i: (b, p, i)),
            pl.BlockSpec((seq, width), lambda b, p, i: (b, p)),
            pl.BlockSpec((None, width, seq), lambda b, p, i: (b, p, 0)),
            pl.BlockSpec((nb, width), lambda b, p, i: (b, p)),
        ],
        out_specs=pl.BlockSpec((MOBA_BLOCK, width), lambda b, p, i: (b * nb + i, p)),
        out_shape=jax.ShapeDtypeStruct((bsz * seq, B_WIDTH), F32),
        scratch_shapes=[pltpu.VMEM((MOBA_HEADS_PER_STEP, nb, MOBA_BLOCK), F32)],
        compiler_params=pltpu.CompilerParams(dimension_semantics=("arbitrary",) * 3, vmem_limit_bytes=VMEM_LIMIT),
        name="moba_p",
    )(qt, k, vt, kmean)


SAMPLE_BLOCKS_PER_STEP = 8
PAGES_PER_BLOCK = MOBA_BLOCK // PAGE_SIZE
PAGE_SLOTS = 3


def _diag_fold(o_all):
    rows = o_all.shape[0]
    n_new = rows // B_HEADS
    rh = lax.broadcasted_iota(jnp.int32, (rows, B_WIDTH), 0) // n_new
    lh = lax.broadcasted_iota(jnp.int32, (rows, B_WIDTH), 1) // HEAD_DIM
    o = jnp.where(rh == lh, o_all, 0.0)
    out = o[:, 0:LANES]
    for t in range(1, B_WIDTH // LANES):
        out = out + o[:, t * LANES:(t + 1) * LANES]
    return out


def _moba_sample_kernel(n_new, nb_past, n_sel, pt_ref, q_ref, kn_ref, vn_ref, ck_hbm, cv_hbm, o_ref,
                        g_scr, m_scr, l_scr, o_scr, kbuf, vbuf, sems):
    npg = SAMPLE_BLOCKS_PER_STEP * PAGES_PER_BLOCK
    seq_id = pl.program_id(0)
    step = pl.program_id(1)
    nsteps = pl.num_programs(1)
    rows = B_HEADS * n_new

    lin = seq_id * nsteps + step
    total = pl.num_programs(0) * nsteps
    cur = lin % PAGE_SLOTS
    ahead = PAGE_SLOTS - 1

    def page_copies(lin_step):
        b, s, slot = lin_step // nsteps, lin_step % nsteps, lin_step % PAGE_SLOTS
        copies = []
        for j in range(npg):
            page = pt_ref[b, s * npg + j]
            copies.append(pltpu.make_async_copy(ck_hbm.at[page], kbuf.at[slot, j], sems.at[slot]))
            copies.append(pltpu.make_async_copy(cv_hbm.at[page], vbuf.at[slot, j], sems.at[slot]))
        return copies

    for first in range(ahead):
        @pl.when((lin == 0) & (first < total))
        def _():
            for cp in page_copies(lin + first):
                cp.start()

    @pl.when(lin + ahead < total)
    def _():
        for cp in page_copies(lin + ahead):
            cp.start()

    for cp in page_copies(lin):
        cp.wait()
    k_refs = [kbuf.at[cur, j] for j in range(npg)]
    v_refs = [vbuf.at[cur, j] for j in range(npg)]
    rh = lax.broadcasted_iota(jnp.int32, (rows, B_WIDTH), 0) // n_new
    lh = lax.broadcasted_iota(jnp.int32, (rows, B_WIDTH), 1) // HEAD_DIM
    q = q_ref[...]
    qbd = jnp.where(rh == lh, jnp.concatenate([q] * B_HEADS, axis=0), 0.0).astype(BF16)
    lane = lax.broadcasted_iota(jnp.int32, (rows, LANES), 1)

    @pl.when(step == 0)
    def _():
        g_scr[...] = jnp.zeros_like(g_scr)
        m_scr[...] = jnp.zeros_like(m_scr)
        l_scr[...] = jnp.zeros_like(l_scr)

    s_pages = [_dot(qbd, k_refs[j][...].astype(BF16)) for j in range(npg)]
    probs, gates, maxes, sums = [], [], [], []
    for bi in range(SAMPLE_BLOCKS_PER_STEP):
        blk_pages = s_pages[bi * PAGES_PER_BLOCK:(bi + 1) * PAGES_PER_BLOCK]
        raw = blk_pages[0]
        mx = blk_pages[0]
        for sp in blk_pages[1:]:
            raw = raw + sp
            mx = jnp.maximum(mx, sp)
        gates.append(jnp.sum(raw, axis=1, keepdims=True) * (1.0 / MOBA_BLOCK))
        m_n = jnp.max(mx, axis=1, keepdims=True) * SCALE
        l_n = jnp.zeros((rows, 1), F32)
        for sp in blk_pages:
            p = jnp.exp(sp * SCALE - m_n)
            l_n = l_n + jnp.sum(p, axis=1, keepdims=True)
            probs.append(p.astype(BF16))
        maxes.append(m_n)
        sums.append(l_n)
    outs = []
    for bi in range(SAMPLE_BLOCKS_PER_STEP):
        o_all = _dot_t(probs[bi * PAGES_PER_BLOCK], v_refs[bi * PAGES_PER_BLOCK][...].astype(BF16))
        for pg in range(1, PAGES_PER_BLOCK):
            j = bi * PAGES_PER_BLOCK + pg
            o_all = o_all + _dot_t(probs[j], v_refs[j][...].astype(BF16))
        outs.append(o_all)
    g_new, m_new, l_new = g_scr[0:rows], m_scr[0:rows], l_scr[0:rows]
    for bi in range(SAMPLE_BLOCKS_PER_STEP):
        n = step * SAMPLE_BLOCKS_PER_STEP + bi
        o_scr[n] = _diag_fold(outs[bi])
        g_new = jnp.where(lane == n_new + n, gates[bi], g_new)
        m_new = jnp.where(lane == n_new + n, maxes[bi], m_new)
        l_new = jnp.where(lane == n_new + n, sums[bi], l_new)
    g_scr[0:rows] = g_new
    m_scr[0:rows] = m_new
    l_scr[0:rows] = l_new

    @pl.when(step == pl.num_programs(1) - 1)
    def _():
        ent = lax.broadcasted_iota(jnp.int32, (LANES, LANES), 0)
        col = lax.broadcasted_iota(jnp.int32, (LANES, LANES), 1)
        is_own = ent < n_new
        is_blk = (ent >= n_new) & (ent < n_new + nb_past)
        gate = jnp.where(is_blk, g_scr[...].T, -jnp.inf)
        sel = jnp.zeros((LANES, LANES), jnp.bool_)
        for _ in range(n_sel):
            mx = jnp.max(gate, axis=0, keepdims=True)
            idx = jnp.min(jnp.where(gate == mx, ent, LANES), axis=0, keepdims=True)
            hit = ent == idx
            sel = sel | hit
            gate = jnp.where(hit, -jnp.inf, gate)
        qpad = jnp.concatenate([qbd, jnp.zeros((LANES - rows, B_WIDTH), BF16)], axis=0)
        s_own = _dot_t(kn_ref[...].astype(BF16), qpad) * SCALE
        st = jnp.concatenate([s_own, m_scr[...].T[n_new:]], axis=0)
        valid = (is_own & (ent <= col % n_new)) | (is_blk & sel)
        st = jnp.where(valid, st, NEG)
        m = jnp.max(st, axis=0, keepdims=True)
        w = jnp.where(valid, jnp.exp(st - m), 0.0)
        den = jnp.sum(w * jnp.where(is_own, 1.0, l_scr[...].T), axis=0, keepdims=True)
        wn = (w / den).T[0:rows]
        num = _diag_fold(_dot(wn[:, 0:n_new].astype(BF16), vn_ref[...].astype(BF16)))
        for nblk in range(nb_past):
            num = num + wn[:, n_new + nblk:n_new + nblk + 1] * o_scr[nblk]
        out = jnp.where(rh == lh, jnp.concatenate([num] * (B_WIDTH // LANES), axis=1), 0.0)
        o_ref[...] = jnp.sum(out.reshape(B_HEADS, n_new, B_WIDTH), axis=0)


def _moba_sample(q, k_new, v_new, cache_kt, cache_vt, page_table, past_len):
    db, n_new, _ = q.shape
    assert past_len % MOBA_BLOCK == 0
    nb_past = past_len // MOBA_BLOCK
    assert nb_past % SAMPLE_BLOCKS_PER_STEP == 0 and n_new % 8 == 0
    assert n_new + nb_past <= LANES and B_HEADS * n_new <= LANES
    n_sel = min(MOBA_TOPK, nb_past)
    nsteps = nb_past // SAMPLE_BLOCKS_PER_STEP
    npg = SAMPLE_BLOCKS_PER_STEP * PAGES_PER_BLOCK
    rows = B_HEADS * n_new

    tok_spec = pl.BlockSpec((None, n_new, B_WIDTH), lambda b, s, pt: (b, 0, 0))
    any_spec = pl.BlockSpec(memory_space=pl.ANY)
    page_buf = pltpu.VMEM((PAGE_SLOTS, npg, B_WIDTH, PAGE_SIZE), F32)
    grid_spec = pltpu.PrefetchScalarGridSpec(
        num_scalar_prefetch=1,
        grid=(db, nsteps),
        in_specs=[tok_spec] * 3 + [any_spec] * 2,
        out_specs=tok_spec,
        scratch_shapes=[pltpu.VMEM((LANES, LANES), F32)] * 3 + [pltpu.VMEM((nb_past, rows, LANES), F32)]
        + [page_buf, page_buf, pltpu.SemaphoreType.DMA((PAGE_SLOTS,))],
    )
    return pl.pallas_call(
        functools.partial(_moba_sample_kernel, n_new, nb_past, n_sel),
        grid_spec=grid_spec,
        out_shape=jax.ShapeDtypeStruct((db, n_new, B_WIDTH), F32),
        compiler_params=pltpu.CompilerParams(dimension_semantics=("arbitrary",) * 2, vmem_limit_bytes=VMEM_LIMIT),
        name="moba_s",
    )(page_table, q, k_new, v_new, cache_kt, cache_vt)


def _merge_kernel(x_ref, pa_ref, gb_ref, yb_ref, wpb_ref, wo_ref, g2_ref, wr_ref, br_ref,
                  h_ref, xn_ref, route_ref, pos_ref, cnt_ref, carry_ref):
    mix = pa_ref[...] + gb_ref[...] * _dot(yb_ref[...].astype(BF16), wpb_ref[...])
    h = x_ref[...] + _dot(mix.astype(BF16), wo_ref[...])
    h_ref[...] = h
    ms = jnp.mean(h * h, axis=-1, keepdims=True)
    xn = (h * lax.rsqrt(ms + EPS)) * g2_ref[...]
    xn_ref[...] = xn
    logits = _dot3(xn, wr_ref[...]) + br_ref[...]
    lane = lax.broadcasted_iota(jnp.int32, logits.shape, 1)
    is_grp = (lane >= N_EXPERTS) & (lane < N_EXPERTS + MOE_GROUPS)
    gl = jnp.where(is_grp, logits, -jnp.inf)
    gmax = jnp.max(gl, axis=1, keepdims=True)
    grp = jnp.min(jnp.where(gl == gmax, lane, LANES), axis=1, keepdims=True) - N_EXPERTS
    p_grp = 1.0 / jnp.sum(jnp.exp(gl - gmax), axis=1, keepdims=True)
    el = jnp.where((lane // MOE_PER_GROUP == grp) & (lane < N_EXPERTS), logits, -jnp.inf)
    v1 = jnp.max(el, axis=1, keepdims=True)
    i1 = jnp.min(jnp.where(el == v1, lane, LANES), axis=1, keepdims=True)
    el2 = jnp.where(lane == i1, -jnp.inf, el)
    v2 = jnp.max(el2, axis=1, keepdims=True)
    i2 = jnp.min(jnp.where(el2 == v2, lane, LANES), axis=1, keepdims=True)
    e2 = jnp.exp(v2 - v1)
    w1 = p_grp / (1.0 + e2)
    w2 = p_grp * e2 / (1.0 + e2)
    route = jnp.where(lane == 0, i1.astype(F32), jnp.where(lane == 1, i2.astype(F32), 0.0))
    route = jnp.where(lane == 2, w1, jnp.where(lane == 3, w2, route))
    route_ref[...] = route
    _rank_rows(route, pos_ref, cnt_ref, carry_ref)


def _merge(x2d, pa, gb, yb, prm):
    n = x2d.shape[0]
    nt = n // ROW_TILE
    full = lambda a: pl.BlockSpec(a.shape, lambda i: (0,) * a.ndim)
    rows = lambda w: pl.BlockSpec((ROW_TILE, w), lambda i: (i, 0))
    consts = [prm['wpb'], prm['wo'], prm['g2'], prm['wr'], prm['br']]
    return pl.pallas_call(
        _merge_kernel,
        grid=(nt,),
        in_specs=[rows(D_MODEL), rows(D_MODEL), rows(D_MODEL), rows(B_WIDTH)] + [full(a) for a in consts],
        out_specs=[rows(D_MODEL), rows(D_MODEL), rows(LANES),
                   pl.BlockSpec((None, POS_ROWS, ROW_TILE), lambda i: (i, 0, 0)), pl.BlockSpec((1, LANES), lambda i: (0, 0))],
        out_shape=[jax.ShapeDtypeStruct((n, D_MODEL), F32), jax.ShapeDtypeStruct((n, D_MODEL), F32),
                   jax.ShapeDtypeStruct((n, LANES), F32),
                   jax.ShapeDtypeStruct((nt, POS_ROWS, ROW_TILE), jnp.int32), jax.ShapeDtypeStruct((1, LANES), jnp.int32)],
        scratch_shapes=[pltpu.VMEM((1, LANES), F32)],
        compiler_params=pltpu.CompilerParams(dimension_semantics=("arbitrary",), vmem_limit_bytes=VMEM_LIMIT),
        name="merge",
    )(x2d, pa, gb, yb, *consts)


EXPERT_TILE = 512
POS_ROWS = 8
DMA_UNROLL = 8


def _rank_rows(route, pos_ref, cnt_ref, carry_ref):
    @pl.when(pl.program_id(0) == 0)
    def _():
        carry_ref[...] = jnp.zeros_like(carry_ref)

    tm = route.shape[0]
    lane = lax.broadcasted_iota(jnp.int32, (tm, LANES), 1)
    lanef = lane.astype(F32)
    i1, i2 = route[:, 0:1], route[:, 1:2]
    hit1, hit2 = lanef == i1, lanef == i2
    onehot = jnp.where(hit1 | hit2, 1.0, 0.0)
    earlier = lax.broadcasted_iota(jnp.int32, (tm, tm), 1) < lax.broadcasted_iota(jnp.int32, (tm, tm), 0)
    before = _dot(jnp.where(earlier, 1.0, 0.0).astype(BF16), onehot.astype(BF16)) + carry_ref[...]
    r1 = jnp.sum(jnp.where(hit1, before, 0.0), axis=1, keepdims=True)
    r2 = jnp.sum(jnp.where(hit2, before, 0.0), axis=1, keepdims=True)
    carry_ref[...] = carry_ref[...] + jnp.sum(onehot, axis=0, keepdims=True)
    packed = jnp.where(lane == 0, i1, jnp.where(lane == 1, r1, jnp.where(lane == 2, i2, jnp.where(lane == 3, r2, 0.0))))
    pos_ref[...] = packed.T[0:POS_ROWS].astype(jnp.int32)
    cnt_ref[...] = carry_ref[...].astype(jnp.int32)


def _dispatch_kernel(off_ref, cnt_ref, pos_ref, xn_ref, xs_hbm, zero_ref, zsem, sem):
    i = pl.program_id(0)
    tm = pos_ref.shape[1]

    last = N_EXPERTS - 1
    tg = zero_ref.shape[0]
    n_act = off_ref[last] // tg + (cnt_ref[last] + tg - 1) // tg
    kmax = xs_hbm.shape[0] // tg
    fills = [(cnt_ref[e] % tg != 0, off_ref[e] + (cnt_ref[e] // tg) * tg) for e in range(N_EXPERTS)]
    fills += [(n_act + j < kmax, (n_act + j) * tg) for j in range(N_EXPERTS)]

    def zero_copy(first):
        return pltpu.make_async_copy(zero_ref, xs_hbm.at[pl.ds(pl.multiple_of(first, tg), tg)], zsem)

    @pl.when(i == 0)
    def _():
        zero_ref[...] = jnp.zeros_like(zero_ref)
        for needed, first in fills:
            @pl.when(needed)
            def _():
                zero_copy(first).start()
        for needed, first in fills:
            @pl.when(needed)
            def _():
                zero_copy(first).wait()

    def row_copies(r):
        src = xn_ref.at[pl.ds(r, 1)]
        return [pltpu.make_async_copy(src, xs_hbm.at[pl.ds(pos_ref[s, r], 1)], sem)
                for s in range(2)]

    def issue(r, c):
        for cp in row_copies(r):
            cp.start()
        return c

    def drain(r, c):
        for cp in row_copies(r):
            cp.wait()
        return c

    lax.fori_loop(0, tm, issue, 0, unroll=DMA_UNROLL)
    lax.fori_loop(0, tm, drain, 0, unroll=DMA_UNROLL)


def _ffn_kernel(te_ref, nact_ref, xs_ref, w1_ref, w3_ref, w2_ref, ys_ref):
    active = pl.program_id(0) < nact_ref[0]

    @pl.when(active)
    def _():
        x = xs_ref[...].astype(BF16)
        hid = jax.nn.silu(_dot(x, w1_ref[...].astype(BF16))) * _dot(x, w3_ref[...].astype(BF16))
        ys_ref[...] = _dot(hid.astype(BF16), w2_ref[...].astype(BF16))

    @pl.when(jnp.logical_not(active))
    def _():
        ys_ref[...] = jnp.zeros_like(ys_ref)


def _combine_kernel(pos_ref, h_ref, route_ref, ys_hbm, o_ref, g1_ref, g2_ref, sem):
    tm = pos_ref.shape[1]
    bufs = (g1_ref, g2_ref)

    def row_copies(r):
        return [pltpu.make_async_copy(ys_hbm.at[pl.ds(pos_ref[s, r], 1)],
                                      bufs[s].at[pl.ds(r, 1)], sem) for s in range(2)]

    def issue(r, c):
        for cp in row_copies(r):
            cp.start()
        return c

    def drain(r, c):
        for cp in row_copies(r):
            cp.wait()
        return c

    lax.fori_loop(0, tm, issue, 0, unroll=DMA_UNROLL)
    lax.fori_loop(0, tm, drain, 0, unroll=DMA_UNROLL)
    route = route_ref[...]
    o_ref[...] = h_ref[...] + route[:, 2:3] * g1_ref[...] + route[:, 3:4] * g2_ref[...]


def _moe(h, xn, route, pos, cnt, w1, w3, w2):
    n = xn.shape[0]
    tm = ROW_TILE
    nt = n // tm
    tg = min(EXPERT_TILE, max(ROW_TILE, n // N_EXPERTS))
    kmax = 2 * n // tg + N_EXPERTS
    cparams = pltpu.CompilerParams(dimension_semantics=("arbitrary",), vmem_limit_bytes=VMEM_LIMIT)

    cnt = cnt[0, :N_EXPERTS]
    ntile = (cnt + tg - 1) // tg
    cum = jnp.cumsum(ntile)
    off = ((cum - ntile) * tg).astype(jnp.int32)
    nact = cum[-1:].astype(jnp.int32)
    tile = jnp.minimum(jnp.arange(kmax, dtype=jnp.int32), nact[0] - 1)
    tile_e = jnp.sum(tile[:, None] >= cum[None, :], axis=1).astype(jnp.int32)
    seg_off = lambda e: jnp.sum(jnp.where(e[..., None] == jnp.arange(N_EXPERTS), off, 0), axis=-1)
    dest = jnp.concatenate([seg_off(pos[:, 0:1]) + pos[:, 1:2], seg_off(pos[:, 2:3]) + pos[:, 3:4], pos[:, 2:]], axis=1)

    pos_spec = pl.BlockSpec((None, POS_ROWS, tm), lambda i, *_: (i, 0, 0), memory_space=pltpu.SMEM)
    any_spec = pl.BlockSpec(memory_space=pl.ANY)
    xs = pl.pallas_call(
        _dispatch_kernel,
        grid_spec=pltpu.PrefetchScalarGridSpec(
            num_scalar_prefetch=2, grid=(nt,),
            in_specs=[pos_spec, pl.BlockSpec((tm, D_MODEL), lambda i, *_: (i, 0))], out_specs=any_spec,
            scratch_shapes=[pltpu.VMEM((tg, D_MODEL), F32), pltpu.SemaphoreType.DMA, pltpu.SemaphoreType.DMA]),
        out_shape=jax.ShapeDtypeStruct((kmax * tg, D_MODEL), F32),
        compiler_params=cparams, name="moe_dispatch",
    )(off, cnt, dest, xn)

    act = lambda k, te, na: (jnp.minimum(k, na[0] - 1), 0)
    ys = pl.pallas_call(
        _ffn_kernel,
        grid_spec=pltpu.PrefetchScalarGridSpec(
            num_scalar_prefetch=2, grid=(kmax,),
            in_specs=[pl.BlockSpec((tg, D_MODEL), act),
                      pl.BlockSpec((None, D_MODEL, D_EXPERT), lambda k, te, na: (te[k], 0, 0)),
                      pl.BlockSpec((None, D_MODEL, D_EXPERT), lambda k, te, na: (te[k], 0, 0)),
                      pl.BlockSpec((None, D_EXPERT, D_MODEL), lambda k, te, na: (te[k], 0, 0))],
            out_specs=pl.BlockSpec((tg, D_MODEL), lambda k, te, na: (k, 0))),
        out_shape=jax.ShapeDtypeStruct((kmax * tg, D_MODEL), F32),
        compiler_params=cparams, name="moe_ffn",
    )(tile_e, nact, xs, w1, w3, w2)

    rows = lambda w: pl.BlockSpec((tm, w), lambda i, *_: (i, 0))
    return pl.pallas_call(
        _combine_kernel,
        grid_spec=pltpu.PrefetchScalarGridSpec(
            num_scalar_prefetch=0, grid=(nt,),
            in_specs=[pos_spec, rows(D_MODEL), rows(LANES), any_spec],
            out_specs=rows(D_MODEL),
            scratch_shapes=[pltpu.VMEM((tm, D_MODEL), F32), pltpu.VMEM((tm, D_MODEL), F32), pltpu.SemaphoreType.DMA]),
        out_shape=jax.ShapeDtypeStruct((n, D_MODEL), F32),
        compiler_params=cparams, name="moe_combine",
    )(dest, h, route, ys)


def _prepare(norm1_g, w_in, b_gate, ln_v_g, ln_v_b, qn_g, kn_g, w_pa, w_pb, w_o, norm2_g,
             w_rg, b_rg, w_re, b_re):
    seg = jnp.kron(jnp.eye(B_HEADS, dtype=F32), jnp.full((HEAD_DIM, HEAD_DIM), 1.0 / HEAD_DIM, F32))
    pad = LANES - N_EXPERTS - MOE_GROUPS
    return dict(
        g1=norm1_g.reshape(1, D_MODEL), win=w_in.astype(BF16), bgate=b_gate.reshape(1, 2 * D_MODEL),
        lng=ln_v_g.reshape(1, A_WIDTH), lnb=ln_v_b.reshape(1, A_WIDTH), seg=seg.astype(BF16),
        qg=jnp.tile(qn_g, B_HEADS).reshape(1, B_WIDTH), kg=jnp.tile(kn_g, B_HEADS).reshape(1, B_WIDTH),
        wpa=w_pa.astype(BF16), wpb=w_pb.astype(BF16), wo=w_o.astype(BF16), g2=norm2_g.reshape(1, D_MODEL),
        wr=jnp.pad(jnp.concatenate([w_re, w_rg], axis=1), ((0, 0), (0, pad))),
        br=jnp.pad(jnp.concatenate([b_re, b_rg]), (0, pad)).reshape(1, LANES),
    )


def _spatial_weights(w_sp, b_sp, length):
    reps = CHUNK // length
    tril = jnp.tril(jnp.ones((length, length), F32))
    w = w_sp[:, :length, :length] * tril
    w = jax.vmap(lambda m: jnp.kron(jnp.eye(reps, dtype=F32), m))(w)
    bias = jnp.tile(b_sp[:, :length].T, (reps, 1))
    side_by_side = jnp.transpose(w, (1, 0, 2)).reshape(CHUNK, A_GROUPS * CHUNK)
    return side_by_side.astype(BF16), jnp.repeat(bias, A_GROUP_DIM, axis=1)


def kernel(x_prompt, x_sample, cache_k, cache_v, page_table, norm1_g, w_in, b_gate, ln_v_g, ln_v_b, w_sp, b_sp,
           qn_g, kn_g, w_pa, w_pb, w_o, norm2_g, w_rg, b_rg, w_re, b_re, w1, w3, w2):
    bsz, seq, _ = x_prompt.shape
    db, n_new, _ = x_sample.shape
    past_len = page_table.shape[1] * PAGE_SIZE
    prm = _prepare(norm1_g, w_in, b_gate, ln_v_g, ln_v_b, qn_g, kn_g, w_pa, w_pb, w_o, norm2_g,
                   w_rg, b_rg, w_re, b_re)

    xp = x_prompt.reshape(bsz * seq, D_MODEL)
    wst, bsp = _spatial_weights(w_sp, b_sp, CHUNK)
    pa, gb, k_b, qt, kt, vt, vt_b, kmean = _mixer(xp, prm, wst, bsp, True, seq)
    yb = _moba_prompt(qt, k_b, vt_b, kmean.reshape(-1, B_WIDTH), bsz, seq)
    y_p = _moe(*_merge(xp, pa, gb, yb, prm), w1, w3, w2)

    xs = x_sample.reshape(db * n_new, D_MODEL)
    wst, bsp = _spatial_weights(w_sp, b_sp, n_new)
    pa, gb, q_s, k_s, v_s, va_s = _mixer(xs, prm, wst, bsp, False)
    n_phys = cache_k.shape[0]
    ckt = jnp.transpose(cache_k, (0, 2, 3, 1)).reshape(n_phys, B_WIDTH, PAGE_SIZE)
    cvt = jnp.transpose(cache_v, (0, 2, 3, 1)).reshape(n_phys, B_WIDTH, PAGE_SIZE)
    tok3 = lambda a: a.reshape(db, n_new, B_WIDTH)
    yb = _moba_sample(tok3(q_s), tok3(k_s), tok3(v_s), ckt, cvt, page_table, past_len)
    y_s = _moe(*_merge(xs, pa, gb, yb.reshape(db * n_new, B_WIDTH), prm), w1, w3, w2)

    heads = lambda a, lead: a.reshape(*lead, B_HEADS, HEAD_DIM)
    from_t = lambda a: jnp.transpose(a.reshape(bsz, B_HEADS, HEAD_DIM, seq), (0, 3, 1, 2))
    return (y_p.reshape(bsz, seq, D_MODEL), y_s.reshape(db, n_new, D_MODEL),
            from_t(kt), from_t(vt),
            heads(k_s, (db, n_new)), heads(v_s, (db, n_new)),
            va_s.reshape(db, n_new, A_WIDTH))
```

```python
import functools

import jax
import jax.numpy as jnp
from jax import lax
from jax.experimental import pallas as pl
from jax.experimental.pallas import tpu as pltpu

F32 = jnp.float32
BF16 = jnp.bfloat16

D_MODEL = 1024
A_GROUPS = 8
A_GROUP_DIM = 64
A_WIDTH = A_GROUPS * A_GROUP_DIM
CHUNK = 128
B_HEADS = 8
HEAD_DIM = 64
B_WIDTH = B_HEADS * HEAD_DIM
MOBA_BLOCK = 256
MOBA_TOPK = 3
PAGE_SIZE = 128
SCALE = HEAD_DIM ** -0.5
MOE_GROUPS = 4
MOE_PER_GROUP = 4
N_EXPERTS = MOE_GROUPS * MOE_PER_GROUP
D_EXPERT = 512
EPS = 1e-6
OFF_U = 0
OFF_VA = OFF_U + A_WIDTH
OFF_Q = OFF_VA + A_WIDTH
OFF_K = OFF_Q + B_WIDTH
OFF_VB = OFF_K + B_WIDTH
OFF_GATE = OFF_VB + B_WIDTH

LANES = 128
MXU_COLS = 256
ROW_TILE = 256
PAIR = 2 * HEAD_DIM
VMEM_LIMIT = 56 * 1024 * 1024
NEG = -1e30


def _dot(a, b):
    return jnp.dot(a, b, preferred_element_type=F32)


def _dot_t(a, b):
    return lax.dot_general(a, b, (((1,), (1,)), ((), ())), preferred_element_type=F32)


def _split(a):
    hi = a.astype(BF16)
    lo = (a - hi.astype(F32)).astype(BF16)
    return hi, lo


def _dot3(a, b):
    ah, al = _split(a)
    bh, bl = _split(b)
    return _dot(ah, bh) + _dot(ah, bl) + _dot(al, bh)


def _mixer_kernel(prompt, x_ref, g1_ref, win_ref, bgate_ref, lng_ref, lnb_ref, seg_ref, wst_ref, bsp_ref,
                  qg_ref, kg_ref, wpa_ref, *outs):
    if prompt:
        pa_ref, gb_ref, kb_ref, qt_ref, kt_ref, vt_ref, vtb_ref, km_ref = outs
    else:
        pa_ref, gb_ref, q_ref, k_ref, v_ref, va_ref = outs
    x = x_ref[...]
    ms = jnp.mean(x * x, axis=-1, keepdims=True)
    xb = ((x * lax.rsqrt(ms + EPS)) * g1_ref[...]).astype(BF16)
    seg = seg_ref[...]

    def proj(off, width):
        return _dot(xb, win_ref[:, off:off + width])

    def segmean(a):
        return _dot(a.astype(BF16), seg)

    u = jax.nn.gelu(proj(OFF_U, A_WIDTH))
    va = jax.nn.gelu(proj(OFF_VA, A_WIDTH))
    xc = va - segmean(va)
    va = xc * lax.rsqrt(segmean(xc * xc) + EPS) * lng_ref[...] + lnb_ref[...]
    if not prompt:
        va_ref[...] = va

    gpt = MXU_COLS // A_GROUP_DIM
    grp = lax.broadcasted_iota(jnp.int32, (CHUNK, MXU_COLS), 1) // A_GROUP_DIM
    vb = va.astype(BF16)
    ya = []
    for c in range(ROW_TILE // CHUNK):
        rows = slice(c * CHUNK, (c + 1) * CHUNK)
        mixed = []
        for t in range(A_WIDTH // MXU_COLS):
            vt = vb[rows, t * MXU_COLS:(t + 1) * MXU_COLS]
            stack = jnp.concatenate([jnp.where(grp == g, vt, jnp.zeros_like(vt)) for g in range(gpt)], axis=0)
            mixed.append(_dot(wst_ref[:, t * gpt * CHUNK:(t + 1) * gpt * CHUNK], stack))
        ya.append(u[rows] * (jnp.concatenate(mixed, axis=1) + bsp_ref[...]))
    ya = jnp.concatenate(ya, axis=0)

    gates_a = jax.nn.sigmoid(proj(OFF_GATE, D_MODEL) + bgate_ref[:, :D_MODEL])
    pa_ref[...] = gates_a * _dot(ya.astype(BF16), wpa_ref[...])
    gb_ref[...] = jax.nn.sigmoid(proj(OFF_GATE + D_MODEL, D_MODEL) + bgate_ref[:, D_MODEL:])

    zq = proj(OFF_Q, B_WIDTH)
    q = zq * lax.rsqrt(segmean(zq * zq) + EPS) * qg_ref[...]
    zk = proj(OFF_K, B_WIDTH)
    k = zk * lax.rsqrt(segmean(zk * zk) + EPS) * kg_ref[...]
    v = proj(OFF_VB, B_WIDTH)
    if prompt:
        vt = v.T
        kb_ref[...] = k.astype(BF16)
        qt_ref[...] = q.T
        kt_ref[...] = k.T
        vt_ref[...] = vt
        vtb_ref[...] = vt.astype(BF16)
        km_ref[0] = jnp.mean(k, axis=0, keepdims=True)
    else:
        k_ref[...] = k
        v_ref[...] = v
        q_ref[...] = q


def _mixer(x2d, prm, wst, bsp, prompt, seq=None):
    n = x2d.shape[0]
    assert n % ROW_TILE == 0
    nt = n // ROW_TILE
    full = lambda a: pl.BlockSpec(a.shape, lambda i: (0,) * a.ndim)
    rows = lambda w: pl.BlockSpec((ROW_TILE, w), lambda i: (i, 0))
    consts = [prm['g1'], prm['win'], prm['bgate'], prm['lng'], prm['lnb'], prm['seg'], wst, bsp,
              prm['qg'], prm['kg'], prm['wpa']]
    out_shape = [jax.ShapeDtypeStruct((n, D_MODEL), F32), jax.ShapeDtypeStruct((n, D_MODEL), F32)]
    out_specs = [rows(D_MODEL), rows(D_MODEL)]
    if prompt:
        tps = seq // ROW_TILE
        out_shape += [jax.ShapeDtypeStruct((n, B_WIDTH), BF16)]
        out_specs += [rows(B_WIDTH)]
        out_shape += [jax.ShapeDtypeStruct((n // seq, B_WIDTH, seq), dt) for dt in (F32, F32, F32, BF16)]
        out_specs += [pl.BlockSpec((None, B_WIDTH, ROW_TILE), lambda i: (i // tps, 0, i % tps))] * 4
        out_shape += [jax.ShapeDtypeStruct((nt, 1, B_WIDTH), F32)]
        out_specs += [pl.BlockSpec((1, 1, B_WIDTH), lambda i: (i, 0, 0))]
    else:
        out_shape += [jax.ShapeDtypeStruct((n, B_WIDTH), F32)] * 3
        out_specs += [rows(B_WIDTH)] * 3
        out_shape += [jax.ShapeDtypeStruct((n, A_WIDTH), F32)]
        out_specs += [rows(A_WIDTH)]
    return pl.pallas_call(
        functools.partial(_mixer_kernel, prompt),
        grid=(nt,),
        in_specs=[rows(D_MODEL)] + [full(a) for a in consts],
        out_specs=out_specs,
        out_shape=out_shape,
        compiler_params=pltpu.CompilerParams(dimension_semantics=("arbitrary",), vmem_limit_bytes=VMEM_LIMIT),
        name="mixer_p" if prompt else "mixer_s",
    )(x2d, *consts)


LOG2E = 1.4426950408889634


MOBA_HEADS_PER_STEP = 8


def _moba_prompt_kernel(n_sel, qt_ref, k_ref, vt_ref, km_ref, o_ref, bias_ref):
    i = pl.program_id(2)
    blk = MOBA_BLOCK
    nb = km_ref.shape[0]
    heads = range(MOBA_HEADS_PER_STEP)
    row = lax.broadcasted_iota(jnp.int32, (PAIR, blk), 0)
    brow = lax.broadcasted_iota(jnp.int32, (nb, blk), 0)
    lanes = [slice((hh // 2) * PAIR, (hh // 2 + 1) * PAIR) for hh in heads]
    hrows = [slice(hh * HEAD_DIM, (hh + 1) * HEAD_DIM) for hh in heads]
    qs = []
    for hh in heads:
        qm = jnp.where((row // HEAD_DIM) == hh % 2, qt_ref[lanes[hh], :], 0.0)
        gate = jnp.where(brow < i, _dot3(km_ref[:, lanes[hh]], qm), -jnp.inf)
        bias = jnp.full((nb, blk), NEG, F32)
        for r in range(n_sel):
            mx = jnp.max(gate, axis=0, keepdims=True)
            idx = jnp.min(jnp.where(gate == mx, brow, nb), axis=0, keepdims=True)
            hit = brow == idx
            bias = jnp.where(hit & (r < i), 0.0, bias)
            gate = jnp.where(hit, -jnp.inf, gate)
        bias_ref[hh] = bias
        qs.append((qm * (SCALE * LOG2E)).astype(BF16))

    own = pl.multiple_of(i * blk, blk)
    kpos = lax.broadcasted_iota(jnp.int32, (blk, blk), 0)
    qpos = lax.broadcasted_iota(jnp.int32, (blk, blk), 1)
    s_own = [_dot(k_ref[pl.ds(own, blk), lanes[hh]], qs[hh]) for hh in heads]
    state = []
    for hh in heads:
        s = jnp.where(kpos <= qpos, s_own[hh], NEG)
        m = jnp.max(s, axis=0, keepdims=True)
        p = jnp.exp2(s - m)
        l = jnp.sum(p, axis=0, keepdims=True)
        acc = _dot(vt_ref[hrows[hh], pl.ds(own, blk)], p.astype(BF16))
        state.append((m, l, acc))

    def body(c, carry):
        off = pl.multiple_of(c * (2 * blk), 2 * blk)
        s2 = [_dot(k_ref[pl.ds(off, 2 * blk), lanes[hh]], qs[hh]) for hh in heads]
        soft = []
        for hh in heads:
            m, l, acc = carry[hh]
            s_lo = s2[hh][:blk] + bias_ref[hh, pl.ds(2 * c, 1), :]
            s_hi = s2[hh][blk:] + bias_ref[hh, pl.ds(2 * c + 1, 1), :]
            m_new = jnp.maximum(m, jnp.maximum(jnp.max(s_lo, axis=0, keepdims=True),
                                               jnp.max(s_hi, axis=0, keepdims=True)))
            alpha = jnp.exp2(m - m_new)
            p_lo = jnp.exp2(s_lo - m_new)
            p_hi = jnp.exp2(s_hi - m_new)
            l = alpha * l + jnp.sum(p_lo, axis=0, keepdims=True) + jnp.sum(p_hi, axis=0, keepdims=True)
            soft.append((m_new, l, alpha, jnp.concatenate([p_lo, p_hi], axis=0).astype(BF16)))
        new = []
        for hh in heads:
            m_new, l, alpha, p = soft[hh]
            acc = alpha * carry[hh][2] + _dot(vt_ref[hrows[hh], pl.ds(off, 2 * blk)], p)
            new.append((m_new, l, acc))
        return tuple(new)

    state = lax.fori_loop(0, (i + 1) // 2, body, tuple(state))
    o_ref[...] = jnp.concatenate([acc / l for (_, l, acc) in state], axis=0).T


def _moba_prompt(qt, k, vt, kmean, bsz, seq):
    nb = seq // MOBA_BLOCK
    assert nb % 2 == 0
    n_sel = min(MOBA_TOPK, nb - 1)
    width = MOBA_HEADS_PER_STEP * HEAD_DIM
    return pl.pallas_call(
        functools.partial(_moba_prompt_kernel, n_sel),
        grid=(bsz, B_WIDTH // width, nb),
        in_specs=[
            pl.BlockSpec((None, width, MOBA_BLOCK), lambda b, p, i: (b, p, i)),
            pl.BlockSpec((seq, width), lambda b, p, i: (b, p)),
            pl.BlockSpec((None, width, seq), lambda b, p, i: (b, p, 0)),
            pl.BlockSpec((nb, width), lambda b, p, i: (b, p)),
        ],
        out_specs=pl.BlockSpec((MOBA_BLOCK, width), lambda b, p, i: (b * nb + i, p)),
        out_shape=jax.ShapeDtypeStruct((bsz * seq, B_WIDTH), F32),
        scratch_shapes=[pltpu.VMEM((MOBA_HEADS_PER_STEP, nb, MOBA_BLOCK), F32)],
        compiler_params=pltpu.CompilerParams(dimension_semantics=("arbitrary",) * 3, vmem_limit_bytes=VMEM_LIMIT),
        name="moba_p",
    )(qt, k, vt, kmean)


SAMPLE_BLOCKS_PER_STEP = 8
PAGES_PER_BLOCK = MOBA_BLOCK // PAGE_SIZE
PAGE_SLOTS = 3


def _diag_fold(o_all):
    rows = o_all.shape[0]
    n_new = rows // B_HEADS
    rh = lax.broadcasted_iota(jnp.int32, (rows, B_WIDTH), 0) // n_new
    lh = lax.broadcasted_iota(jnp.int32, (rows, B_WIDTH), 1) // HEAD_DIM
    o = jnp.where(rh == lh, o_all, 0.0)
    out = o[:, 0:LANES]
    for t in range(1, B_WIDTH // LANES):
        out = out + o[:, t * LANES:(t + 1) * LANES]
    return out


def _moba_sample_kernel(n_new, nb_past, n_sel, pt_ref, q_ref, kn_ref, vn_ref, ck_hbm, cv_hbm, o_ref,
                        g_scr, m_scr, l_scr, o_scr, kbuf, vbuf, sems):
    npg = SAMPLE_BLOCKS_PER_STEP * PAGES_PER_BLOCK
    seq_id = pl.program_id(0)
    step = pl.program_id(1)
    nsteps = pl.num_programs(1)
    rows = B_HEADS * n_new

    lin = seq_id * nsteps + step
    total = pl.num_programs(0) * nsteps
    cur = lin % PAGE_SLOTS
    ahead = PAGE_SLOTS - 1

    def page_copies(lin_step):
        b, s, slot = lin_step // nsteps, lin_step % nsteps, lin_step % PAGE_SLOTS
        copies = []
        for j in range(npg):
            page = pt_ref[b, s * npg + j]
            copies.append(pltpu.make_async_copy(ck_hbm.at[page], kbuf.at[slot, j], sems.at[slot]))
            copies.append(pltpu.make_async_copy(cv_hbm.at[page], vbuf.at[slot, j], sems.at[slot]))
        return copies

    for first in range(ahead):
        @pl.when((lin == 0) & (first < total))
        def _():
            for cp in page_copies(lin + first):
                cp.start()

    @pl.when(lin + ahead < total)
    def _():
        for cp in page_copies(lin + ahead):
            cp.start()

    for cp in page_copies(lin):
        cp.wait()
    k_refs = [kbuf.at[cur, j] for j in range(npg)]
    v_refs = [vbuf.at[cur, j] for j in range(npg)]
    rh = lax.broadcasted_iota(jnp.int32, (rows, B_WIDTH), 0) // n_new
    lh = lax.broadcasted_iota(jnp.int32, (rows, B_WIDTH), 1) // HEAD_DIM
    q = q_ref[...]
    qbd = jnp.where(rh == lh, jnp.concatenate([q] * B_HEADS, axis=0), 0.0).astype(BF16)
    lane = lax.broadcasted_iota(jnp.int32, (rows, LANES), 1)

    @pl.when(step == 0)
    def _():
        g_scr[...] = jnp.zeros_like(g_scr)
        m_scr[...] = jnp.zeros_like(m_scr)
        l_scr[...] = jnp.zeros_like(l_scr)

    s_pages = [_dot(qbd, k_refs[j][...].astype(BF16)) for j in range(npg)]
    probs, gates, maxes, sums = [], [], [], []
    for bi in range(SAMPLE_BLOCKS_PER_STEP):
        blk_pages = s_pages[bi * PAGES_PER_BLOCK:(bi + 1) * PAGES_PER_BLOCK]
        raw = blk_pages[0]
        mx = blk_pages[0]
        for sp in blk_pages[1:]:
            raw = raw + sp
            mx = jnp.maximum(mx, sp)
        gates.append(jnp.sum(raw, axis=1, keepdims=True) * (1.0 / MOBA_BLOCK))
        m_n = jnp.max(mx, axis=1, keepdims=True) * SCALE
        l_n = jnp.zeros((rows, 1), F32)
        for sp in blk_pages:
            p = jnp.exp(sp * SCALE - m_n)
            l_n = l_n + jnp.sum(p, axis=1, keepdims=True)
            probs.append(p.astype(BF16))
        maxes.append(m_n)
        sums.append(l_n)
    outs = []
    for bi in range(SAMPLE_BLOCKS_PER_STEP):
        o_all = _dot_t(probs[bi * PAGES_PER_BLOCK], v_refs[bi * PAGES_PER_BLOCK][...].astype(BF16))
        for pg in range(1, PAGES_PER_BLOCK):
            j = bi * PAGES_PER_BLOCK + pg
            o_all = o_all + _dot_t(probs[j], v_refs[j][...].astype(BF16))
        outs.append(o_all)
    g_new, m_new, l_new = g_scr[0:rows], m_scr[0:rows], l_scr[0:rows]
    for bi in range(SAMPLE_BLOCKS_PER_STEP):
        n = step * SAMPLE_BLOCKS_PER_STEP + bi
        o_scr[n] = _diag_fold(outs[bi])
        g_new = jnp.where(lane == n_new + n, gates[bi], g_new)
        m_new = jnp.where(lane == n_new + n, maxes[bi], m_new)
        l_new = jnp.where(lane == n_new + n, sums[bi], l_new)
    g_scr[0:rows] = g_new
    m_scr[0:rows] = m_new
    l_scr[0:rows] = l_new

    @pl.when(step == pl.num_programs(1) - 1)
    def _():
        ent = lax.broadcasted_iota(jnp.int32, (LANES, LANES), 0)
        col = lax.broadcasted_iota(jnp.int32, (LANES, LANES), 1)
        is_own = ent < n_new
        is_blk = (ent >= n_new) & (ent < n_new + nb_past)
        gate = jnp.where(is_blk, g_scr[...].T, -jnp.inf)
        sel = jnp.zeros((LANES, LANES), jnp.bool_)
        for _ in range(n_sel):
            mx = jnp.max(gate, axis=0, keepdims=True)
            idx = jnp.min(jnp.where(gate == mx, ent, LANES), axis=0, keepdims=True)
            hit = ent == idx
            sel = sel | hit
            gate = jnp.where(hit, -jnp.inf, gate)
        qpad = jnp.concatenate([qbd, jnp.zeros((LANES - rows, B_WIDTH), BF16)], axis=0)
        s_own = _dot_t(kn_ref[...].astype(BF16), qpad) * SCALE
        st = jnp.concatenate([s_own, m_scr[...].T[n_new:]], axis=0)
        valid = (is_own & (ent <= col % n_new)) | (is_blk & sel)
        st = jnp.where(valid, st, NEG)
        m = jnp.max(st, axis=0, keepdims=True)
        w = jnp.where(valid, jnp.exp(st - m), 0.0)
        den = jnp.sum(w * jnp.where(is_own, 1.0, l_scr[...].T), axis=0, keepdims=True)
        wn = (w / den).T[0:rows]
        num = _diag_fold(_dot(wn[:, 0:n_new].astype(BF16), vn_ref[...].astype(BF16)))
        for nblk in range(nb_past):
            num = num + wn[:, n_new + nblk:n_new + nblk + 1] * o_scr[nblk]
        out = jnp.where(rh == lh, jnp.concatenate([num] * (B_WIDTH // LANES), axis=1), 0.0)
        o_ref[...] = jnp.sum(out.reshape(B_HEADS, n_new, B_WIDTH), axis=0)


def _moba_sample(q, k_new, v_new, cache_kt, cache_vt, page_table, past_len):
    db, n_new, _ = q.shape
    assert past_len % MOBA_BLOCK == 0
    nb_past = past_len // MOBA_BLOCK
    assert nb_past % SAMPLE_BLOCKS_PER_STEP == 0 and n_new % 8 == 0
    assert n_new + nb_past <= LANES and B_HEADS * n_new <= LANES
    n_sel = min(MOBA_TOPK, nb_past)
    nsteps = nb_past // SAMPLE_BLOCKS_PER_STEP
    npg = SAMPLE_BLOCKS_PER_STEP * PAGES_PER_BLOCK
    rows = B_HEADS * n_new

    tok_spec = pl.BlockSpec((None, n_new, B_WIDTH), lambda b, s, pt: (b, 0, 0))
    any_spec = pl.BlockSpec(memory_space=pl.ANY)
    page_buf = pltpu.VMEM((PAGE_SLOTS, npg, B_WIDTH, PAGE_SIZE), F32)
    grid_spec = pltpu.PrefetchScalarGridSpec(
        num_scalar_prefetch=1,
        grid=(db, nsteps),
        in_specs=[tok_spec] * 3 + [any_spec] * 2,
        out_specs=tok_spec,
        scratch_shapes=[pltpu.VMEM((LANES, LANES), F32)] * 3 + [pltpu.VMEM((nb_past, rows, LANES), F32)]
        + [page_buf, page_buf, pltpu.SemaphoreType.DMA((PAGE_SLOTS,))],
    )
    return pl.pallas_call(
        functools.partial(_moba_sample_kernel, n_new, nb_past, n_sel),
        grid_spec=grid_spec,
        out_shape=jax.ShapeDtypeStruct((db, n_new, B_WIDTH), F32),
        compiler_params=pltpu.CompilerParams(dimension_semantics=("arbitrary",) * 2, vmem_limit_bytes=VMEM_LIMIT),
        name="moba_s",
    )(page_table, q, k_new, v_new, cache_kt, cache_vt)


def _merge_kernel(x_ref, pa_ref, gb_ref, yb_ref, wpb_ref, wo_ref, g2_ref, wr_ref, br_ref,
                  h_ref, xn_ref, route_ref, pos_ref, cnt_ref, carry_ref):
    mix = pa_ref[...] + gb_ref[...] * _dot(yb_ref[...].astype(BF16), wpb_ref[...])
    h = x_ref[...] + _dot(mix.astype(BF16), wo_ref[...])
    h_ref[...] = h
    ms = jnp.mean(h * h, axis=-1, keepdims=True)
    xn = (h * lax.rsqrt(ms + EPS)) * g2_ref[...]
    xn_ref[...] = xn
    logits = _dot3(xn, wr_ref[...]) + br_ref[...]
    lane = lax.broadcasted_iota(jnp.int32, logits.shape, 1)
    is_grp = (lane >= N_EXPERTS) & (lane < N_EXPERTS + MOE_GROUPS)
    gl = jnp.where(is_grp, logits, -jnp.inf)
    gmax = jnp.max(gl, axis=1, keepdims=True)
    grp = jnp.min(jnp.where(gl == gmax, lane, LANES), axis=1, keepdims=True) - N_EXPERTS
    p_grp = 1.0 / jnp.sum(jnp.exp(gl - gmax), axis=1, keepdims=True)
    el = jnp.where((lane // MOE_PER_GROUP == grp) & (lane < N_EXPERTS), logits, -jnp.inf)
    v1 = jnp.max(el, axis=1, keepdims=True)
    i1 = jnp.min(jnp.where(el == v1, lane, LANES), axis=1, keepdims=True)
    el2 = jnp.where(lane == i1, -jnp.inf, el)
    v2 = jnp.max(el2, axis=1, keepdims=True)
    i2 = jnp.min(jnp.where(el2 == v2, lane, LANES), axis=1, keepdims=True)
    e2 = jnp.exp(v2 - v1)
    w1 = p_grp / (1.0 + e2)
    w2 = p_grp * e2 / (1.0 + e2)
    route = jnp.where(lane == 0, i1.astype(F32), jnp.where(lane == 1, i2.astype(F32), 0.0))
    route = jnp.where(lane == 2, w1, jnp.where(lane == 3, w2, route))
    route_ref[...] = route
    _rank_rows(route, pos_ref, cnt_ref, carry_ref)


def _merge(x2d, pa, gb, yb, prm):
    n = x2d.shape[0]
    nt = n // ROW_TILE
    full = lambda a: pl.BlockSpec(a.shape, lambda i: (0,) * a.ndim)
    rows = lambda w: pl.BlockSpec((ROW_TILE, w), lambda i: (i, 0))
    consts = [prm['wpb'], prm['wo'], prm['g2'], prm['wr'], prm['br']]
    return pl.pallas_call(
        _merge_kernel,
        grid=(nt,),
        in_specs=[rows(D_MODEL), rows(D_MODEL), rows(D_MODEL), rows(B_WIDTH)] + [full(a) for a in consts],
        out_specs=[rows(D_MODEL), rows(D_MODEL), rows(LANES),
                   pl.BlockSpec((None, POS_ROWS, ROW_TILE), lambda i: (i, 0, 0)), pl.BlockSpec((1, LANES), lambda i: (0, 0))],
        out_shape=[jax.ShapeDtypeStruct((n, D_MODEL), F32), jax.ShapeDtypeStruct((n, D_MODEL), F32),
                   jax.ShapeDtypeStruct((n, LANES), F32),
                   jax.ShapeDtypeStruct((nt, POS_ROWS, ROW_TILE), jnp.int32), jax.ShapeDtypeStruct((1, LANES), jnp.int32)],
        scratch_shapes=[pltpu.VMEM((1, LANES), F32)],
        compiler_params=pltpu.CompilerParams(dimension_semantics=("arbitrary",), vmem_limit_bytes=VMEM_LIMIT),
        name="merge",
    )(x2d, pa, gb, yb, *consts)


EXPERT_TILE = 512
POS_ROWS = 8
DMA_UNROLL = 8


def _rank_rows(route, pos_ref, cnt_ref, carry_ref):
    @pl.when(pl.program_id(0) == 0)
    def _():
        carry_ref[...] = jnp.zeros_like(carry_ref)

    tm = route.shape[0]
    lane = lax.broadcasted_iota(jnp.int32, (tm, LANES), 1)
    lanef = lane.astype(F32)
    i1, i2 = route[:, 0:1], route[:, 1:2]
    hit1, hit2 = lanef == i1, lanef == i2
    onehot = jnp.where(hit1 | hit2, 1.0, 0.0)
    earlier = lax.broadcasted_iota(jnp.int32, (tm, tm), 1) < lax.broadcasted_iota(jnp.int32, (tm, tm), 0)
    before = _dot(jnp.where(earlier, 1.0, 0.0).astype(BF16), onehot.astype(BF16)) + carry_ref[...]
    r1 = jnp.sum(jnp.where(hit1, before, 0.0), axis=1, keepdims=True)
    r2 = jnp.sum(jnp.where(hit2, before, 0.0), axis=1, keepdims=True)
    carry_ref[...] = carry_ref[...] + jnp.sum(onehot, axis=0, keepdims=True)
    packed = jnp.where(lane == 0, i1, jnp.where(lane == 1, r1, jnp.where(lane == 2, i2, jnp.where(lane == 3, r2, 0.0))))
    pos_ref[...] = packed.T[0:POS_ROWS].astype(jnp.int32)
    cnt_ref[...] = carry_ref[...].astype(jnp.int32)


def _dispatch_kernel(off_ref, cnt_ref, pos_ref, xn_ref, xs_hbm, zero_ref, zsem, sem):
    i = pl.program_id(0)
    tm = pos_ref.shape[1]

    tg = zero_ref.shape[0]
    kmax = xs_hbm.shape[0] // tg

    def zero_copy(first):
        return pltpu.make_async_copy(zero_ref, xs_hbm.at[pl.ds(pl.multiple_of(first, tg), tg)], zsem)

    @pl.when(i == 0)
    def _():
        last = N_EXPERTS - 1
        full_tiles = lambda c: c & ~(tg - 1)
        n_act = (off_ref[last] + full_tiles(cnt_ref[last] + tg - 1)) // tg
        fills = [(cnt_ref[e] & (tg - 1) != 0, off_ref[e] + full_tiles(cnt_ref[e])) for e in range(N_EXPERTS)]
        fills += [(n_act + j < kmax, (n_act + j) * tg) for j in range(N_EXPERTS)]
        zero_ref[...] = jnp.zeros_like(zero_ref)
        for needed, first in fills:
            @pl.when(needed)
            def _():
                zero_copy(first).start()
        for needed, first in fills:
            @pl.when(needed)
            def _():
                zero_copy(first).wait()

    def row_copies(r):
        src = xn_ref.at[pl.ds(r, 1)]
        return [pltpu.make_async_copy(src, xs_hbm.at[pl.ds(pos_ref[s, r], 1)], sem)
                for s in range(2)]

    def issue(r, c):
        for cp in row_copies(r):
            cp.start()
        return c

    def drain(r, c):
        for cp in row_copies(r):
            cp.wait()
        return c

    lax.fori_loop(0, tm, issue, 0, unroll=DMA_UNROLL)
    lax.fori_loop(0, tm, drain, 0, unroll=DMA_UNROLL)


def _ffn_kernel(te_ref, nact_ref, xs_ref, w1_ref, w3_ref, w2_ref, ys_ref):
    active = pl.program_id(0) < nact_ref[0]

    @pl.when(active)
    def _():
        x = xs_ref[...].astype(BF16)
        hid = jax.nn.silu(_dot(x, w1_ref[...].astype(BF16))) * _dot(x, w3_ref[...].astype(BF16))
        ys_ref[...] = _dot(hid.astype(BF16), w2_ref[...].astype(BF16))

    @pl.when(jnp.logical_not(active))
    def _():
        ys_ref[...] = jnp.zeros_like(ys_ref)


def _combine_kernel(pos_ref, h_ref, route_ref, ys_hbm, o_ref, g1_ref, g2_ref, sem):
    tm = pos_ref.shape[1]
    bufs = (g1_ref, g2_ref)

    def row_copies(r):
        return [pltpu.make_async_copy(ys_hbm.at[pl.ds(pos_ref[s, r], 1)],
                                      bufs[s].at[pl.ds(r, 1)], sem) for s in range(2)]

    def issue(r, c):
        for cp in row_copies(r):
            cp.start()
        return c

    def drain(r, c):
        for cp in row_copies(r):
            cp.wait()
        return c

    lax.fori_loop(0, tm, issue, 0, unroll=DMA_UNROLL)
    lax.fori_loop(0, tm, drain, 0, unroll=DMA_UNROLL)
    route = route_ref[...]
    o_ref[...] = h_ref[...] + route[:, 2:3] * g1_ref[...] + route[:, 3:4] * g2_ref[...]


def _moe(h, xn, route, pos, cnt, w1, w3, w2):
    n = xn.shape[0]
    tm = ROW_TILE
    nt = n // tm
    tg = min(EXPERT_TILE, max(ROW_TILE, n // N_EXPERTS))
    assert tg & (tg - 1) == 0 and (2 * n) % tg == 0
    kmax = 2 * n // tg + N_EXPERTS
    cparams = pltpu.CompilerParams(dimension_semantics=("arbitrary",), vmem_limit_bytes=VMEM_LIMIT)

    cnt = cnt[0, :N_EXPERTS]
    ntile = (cnt + tg - 1) // tg
    cum = jnp.cumsum(ntile)
    off = ((cum - ntile) * tg).astype(jnp.int32)
    nact = cum[-1:].astype(jnp.int32)
    tile = jnp.minimum(jnp.arange(kmax, dtype=jnp.int32), nact[0] - 1)
    tile_e = jnp.sum(tile[:, None] >= cum[None, :], axis=1).astype(jnp.int32)
    seg_off = lambda e: jnp.sum(jnp.where(e[..., None] == jnp.arange(N_EXPERTS), off, 0), axis=-1)
    dest = jnp.concatenate([seg_off(pos[:, 0:1]) + pos[:, 1:2], seg_off(pos[:, 2:3]) + pos[:, 3:4], pos[:, 2:]], axis=1)

    pos_spec = pl.BlockSpec((None, POS_ROWS, tm), lambda i, *_: (i, 0, 0), memory_space=pltpu.SMEM)
    any_spec = pl.BlockSpec(memory_space=pl.ANY)
    xs = pl.pallas_call(
        _dispatch_kernel,
        grid_spec=pltpu.PrefetchScalarGridSpec(
            num_scalar_prefetch=2, grid=(nt,),
            in_specs=[pos_spec, pl.BlockSpec((tm, D_MODEL), lambda i, *_: (i, 0))], out_specs=any_spec,
            scratch_shapes=[pltpu.VMEM((tg, D_MODEL), F32), pltpu.SemaphoreType.DMA, pltpu.SemaphoreType.DMA]),
        out_shape=jax.ShapeDtypeStruct((kmax * tg, D_MODEL), F32),
        compiler_params=cparams, name="moe_dispatch",
    )(off, cnt, dest, xn)

    act = lambda k, te, na: (jnp.minimum(k, na[0] - 1), 0)
    ys = pl.pallas_call(
        _ffn_kernel,
        grid_spec=pltpu.PrefetchScalarGridSpec(
            num_scalar_prefetch=2, grid=(kmax,),
            in_specs=[pl.BlockSpec((tg, D_MODEL), act),
                      pl.BlockSpec((None, D_MODEL, D_EXPERT), lambda k, te, na: (te[k], 0, 0)),
                      pl.BlockSpec((None, D_MODEL, D_EXPERT), lambda k, te, na: (te[k], 0, 0)),
                      pl.BlockSpec((None, D_EXPERT, D_MODEL), lambda k, te, na: (te[k], 0, 0))],
            out_specs=pl.BlockSpec((tg, D_MODEL), lambda k, te, na: (k, 0))),
        out_shape=jax.ShapeDtypeStruct((kmax * tg, D_MODEL), F32),
        compiler_params=cparams, name="moe_ffn",
    )(tile_e, nact, xs, w1, w3, w2)

    rows = lambda w: pl.BlockSpec((tm, w), lambda i, *_: (i, 0))
    return pl.pallas_call(
        _combine_kernel,
        grid_spec=pltpu.PrefetchScalarGridSpec(
            num_scalar_prefetch=0, grid=(nt,),
            in_specs=[pos_spec, rows(D_MODEL), rows(LANES), any_spec],
            out_specs=rows(D_MODEL),
            scratch_shapes=[pltpu.VMEM((tm, D_MODEL), F32), pltpu.VMEM((tm, D_MODEL), F32), pltpu.SemaphoreType.DMA]),
        out_shape=jax.ShapeDtypeStruct((n, D_MODEL), F32),
        compiler_params=cparams, name="moe_combine",
    )(dest, h, route, ys)


def _prepare(norm1_g, w_in, b_gate, ln_v_g, ln_v_b, qn_g, kn_g, w_pa, w_pb, w_o, norm2_g,
             w_rg, b_rg, w_re, b_re):
    seg = jnp.kron(jnp.eye(B_HEADS, dtype=F32), jnp.full((HEAD_DIM, HEAD_DIM), 1.0 / HEAD_DIM, F32))
    pad = LANES - N_EXPERTS - MOE_GROUPS
    return dict(
        g1=norm1_g.reshape(1, D_MODEL), win=w_in.astype(BF16), bgate=b_gate.reshape(1, 2 * D_MODEL),
        lng=ln_v_g.reshape(1, A_WIDTH), lnb=ln_v_b.reshape(1, A_WIDTH), seg=seg.astype(BF16),
        qg=jnp.tile(qn_g, B_HEADS).reshape(1, B_WIDTH), kg=jnp.tile(kn_g, B_HEADS).reshape(1, B_WIDTH),
        wpa=w_pa.astype(BF16), wpb=w_pb.astype(BF16), wo=w_o.astype(BF16), g2=norm2_g.reshape(1, D_MODEL),
        wr=jnp.pad(jnp.concatenate([w_re, w_rg], axis=1), ((0, 0), (0, pad))),
        br=jnp.pad(jnp.concatenate([b_re, b_rg]), (0, pad)).reshape(1, LANES),
    )


def _spatial_weights(w_sp, b_sp, length):
    reps = CHUNK // length
    tril = jnp.tril(jnp.ones((length, length), F32))
    w = w_sp[:, :length, :length] * tril
    w = jax.vmap(lambda m: jnp.kron(jnp.eye(reps, dtype=F32), m))(w)
    bias = jnp.tile(b_sp[:, :length].T, (reps, 1))
    side_by_side = jnp.transpose(w, (1, 0, 2)).reshape(CHUNK, A_GROUPS * CHUNK)
    return side_by_side.astype(BF16), jnp.repeat(bias, A_GROUP_DIM, axis=1)


def kernel(x_prompt, x_sample, cache_k, cache_v, page_table, norm1_g, w_in, b_gate, ln_v_g, ln_v_b, w_sp, b_sp,
           qn_g, kn_g, w_pa, w_pb, w_o, norm2_g, w_rg, b_rg, w_re, b_re, w1, w3, w2):
    bsz, seq, _ = x_prompt.shape
    db, n_new, _ = x_sample.shape
    past_len = page_table.shape[1] * PAGE_SIZE
    prm = _prepare(norm1_g, w_in, b_gate, ln_v_g, ln_v_b, qn_g, kn_g, w_pa, w_pb, w_o, norm2_g,
                   w_rg, b_rg, w_re, b_re)

    xp = x_prompt.reshape(bsz * seq, D_MODEL)
    wst, bsp = _spatial_weights(w_sp, b_sp, CHUNK)
    pa, gb, k_b, qt, kt, vt, vt_b, kmean = _mixer(xp, prm, wst, bsp, True, seq)
    yb = _moba_prompt(qt, k_b, vt_b, kmean.reshape(-1, B_WIDTH), bsz, seq)
    y_p = _moe(*_merge(xp, pa, gb, yb, prm), w1, w3, w2)

    xs = x_sample.reshape(db * n_new, D_MODEL)
    wst, bsp = _spatial_weights(w_sp, b_sp, n_new)
    pa, gb, q_s, k_s, v_s, va_s = _mixer(xs, prm, wst, bsp, False)
    n_phys = cache_k.shape[0]
    ckt = jnp.transpose(cache_k, (0, 2, 3, 1)).reshape(n_phys, B_WIDTH, PAGE_SIZE)
    cvt = jnp.transpose(cache_v, (0, 2, 3, 1)).reshape(n_phys, B_WIDTH, PAGE_SIZE)
    tok3 = lambda a: a.reshape(db, n_new, B_WIDTH)
    yb = _moba_sample(tok3(q_s), tok3(k_s), tok3(v_s), ckt, cvt, page_table, past_len)
    y_s = _moe(*_merge(xs, pa, gb, yb.reshape(db * n_new, B_WIDTH), prm), w1, w3, w2)

    heads = lambda a, lead: a.reshape(*lead, B_HEADS, HEAD_DIM)
    from_t = lambda a: jnp.transpose(a.reshape(bsz, B_HEADS, HEAD_DIM, seq), (0, 3, 1, 2))
    return (y_p.reshape(bsz, seq, D_MODEL), y_s.reshape(db, n_new, D_MODEL),
            from_t(kt), from_t(vt),
            heads(k_s, (db, n_new)), heads(v_s, (db, n_new)),
            va_s.reshape(db, n_new, A_WIDTH))
```
